```python
import jax, jax.numpy as jnp
from jax import lax
import numpy as np

D_MODEL = 1024
BATCH = 16
SEQ = 2048
DEPTH = 1
DEC_BATCH = 128
DEC_SEQ = 8
PAST_LEN = 16384
PAGE_SIZE = 128

F32 = jnp.float32
EPS = 1e-6
NEG = -1e30
ROPE_THETA = 500000.0
MLA_HEADS = 8
MLA_NOPE = 64
MLA_ROPE = 32
MLA_V = 64
MLA_Q_RANK = 256
MLA_KV_RANK = 128
MOBA_HEADS = 8
MOBA_HD = 64
MOBA_ROT = MOBA_HD // 4
MOBA_BLOCK = 256
MOBA_TOPK = 3
PEER_HEADS = 8
PEER_NKEYS = 128
PEER_N = PEER_NKEYS * PEER_NKEYS
PEER_DKEY = 256
PEER_TOPK = 16
MLA_WIDTH = MLA_HEADS * MLA_V
MOBA_WIDTH = MOBA_HEADS * MOBA_HD
MIX_WIDTH = MLA_WIDTH + MOBA_WIDTH
IN_SPLITS = (MLA_Q_RANK, MLA_KV_RANK, MLA_ROPE, MOBA_WIDTH, MOBA_WIDTH, MOBA_WIDTH)
IN_WIDTH = MLA_Q_RANK + MLA_KV_RANK + MLA_ROPE + 3 * MOBA_WIDTH
ATTN_Q_BLOCK = 128
MOBA_Q_CHUNK = 16
PEER_CHUNK = 128

kernel_name = 'hymba_mla_moba_peer_adaln_step'


def rms_norm(x, g):
    xf = x.astype(F32)
    y = xf * lax.rsqrt(jnp.mean(xf * xf, axis=-1, keepdims=True) + EPS)
    return y.astype(x.dtype) * g


def rope(x, pos, rot):
    half = rot // 2
    inv = ROPE_THETA ** (-jnp.arange(half, dtype=F32) / half)
    ang = pos[:, None] * inv[None, :]
    cos = jnp.cos(ang)[None, :, None, :].astype(x.dtype)
    sin = jnp.sin(ang)[None, :, None, :].astype(x.dtype)
    x1, x2, rest = x[..., :half], x[..., half:rot], x[..., rot:]
    return jnp.concatenate([x1 * cos - x2 * sin, x2 * cos + x1 * sin, rest], axis=-1)


def project_inputs(h, pos, w_in, g_q, w_uq, g_kv, w_uk):
    B, S, _ = h.shape
    z = h @ w_in
    cuts, acc = [], 0
    for w in IN_SPLITS[:-1]:
        acc += w
        cuts.append(acc)
    cq, ckv_raw, kr_raw, qm, km, vm = jnp.split(z, cuts, axis=-1)
    q = (rms_norm(cq, g_q) @ w_uq).reshape(B, S, MLA_HEADS, MLA_NOPE + MLA_ROPE)
    q_rope = rope(q[..., MLA_NOPE:], pos, MLA_ROPE)
    q_lat = jnp.einsum('bshn,rhn->bshr', q[..., :MLA_NOPE], w_uk)
    ckv = rms_norm(ckv_raw, g_kv)
    krope = rope(kr_raw[:, :, None, :], pos, MLA_ROPE)[:, :, 0, :]
    qm = rope(qm.reshape(B, S, MOBA_HEADS, MOBA_HD), pos, MOBA_ROT)
    km = rope(km.reshape(B, S, MOBA_HEADS, MOBA_HD), pos, MOBA_ROT)
    vm = vm.reshape(B, S, MOBA_HEADS, MOBA_HD)
    return q_lat, q_rope, ckv, krope, qm, km, vm


def mla_core(q_lat, q_rope, ckv, krope, q_pos, k_pos):
    scale = (MLA_NOPE + MLA_ROPE) ** -0.5
    s = (jnp.einsum('bqhr,bkr->bhqk', q_lat, ckv) + jnp.einsum('bqhp,bkp->bhqk', q_rope, krope)).astype(F32) * scale
    s = jnp.where(k_pos[None, :] <= q_pos[:, None], s, NEG)
    p = jax.nn.softmax(s, axis=-1).astype(ckv.dtype)
    return jnp.einsum('bhqk,bkr->bqhr', p, ckv)


def mla_prompt(q_lat, q_rope, ckv, krope):
    B, S, H, R = q_lat.shape
    nqb = S // ATTN_Q_BLOCK
    k_pos = jnp.arange(S)
    qlb = jnp.moveaxis(q_lat.reshape(B, nqb, ATTN_Q_BLOCK, H, R), 1, 0)
    qrb = jnp.moveaxis(q_rope.reshape(B, nqb, ATTN_Q_BLOCK, H, MLA_ROPE), 1, 0)

    def blk(args):
        i, ql, qr = args
        return mla_core(ql, qr, ckv, krope, i * ATTN_Q_BLOCK + jnp.arange(ATTN_Q_BLOCK), k_pos)

    o = lax.map(blk, (jnp.arange(nqb), qlb, qrb))
    return jnp.moveaxis(o, 0, 1).reshape(B, S, H, R)


def mla_sample(q_lat, q_rope, ckv_new, krope_new, cache_ckv, cache_krope, page_table):
    B, T = ckv_new.shape[:2]
    ckv = jnp.concatenate([cache_ckv[page_table].reshape(B, PAST_LEN, MLA_KV_RANK), ckv_new], axis=1)
    krope = jnp.concatenate([cache_krope[page_table].reshape(B, PAST_LEN, MLA_ROPE), krope_new], axis=1)
    return mla_core(q_lat, q_rope, ckv, krope, PAST_LEN + jnp.arange(T), jnp.arange(PAST_LEN + T))


def moba_core(q, k_sel, v_sel, sel_mask, k_own, v_own, own_mask):
    scale = MOBA_HD ** -0.5
    s_own = jnp.einsum('bqhd,bkhd->bqhk', q, k_own).astype(F32) * scale
    s_own = jnp.where(own_mask[None, :, None, :], s_own, NEG)
    if k_sel is None:
        p = jax.nn.softmax(s_own, axis=-1).astype(q.dtype)
        return jnp.einsum('bqhk,bkhd->bqhd', p, v_own)
    s_sel = jnp.einsum('bqhd,bqhkd->bqhk', q, k_sel).astype(F32) * scale
    s_sel = jnp.where(sel_mask, s_sel, NEG)
    ks = s_sel.shape[-1]
    p = jax.nn.softmax(jnp.concatenate([s_sel, s_own], axis=-1), axis=-1).astype(q.dtype)
    return (jnp.einsum('bqhk,bqhkd->bqhd', p[..., :ks], v_sel)
            + jnp.einsum('bqhk,bkhd->bqhd', p[..., ks:], v_own))


def moba_prompt(q, k, v):
    B, S, H, D = q.shape
    nb = -(-S // MOBA_BLOCK)
    pad = nb * MOBA_BLOCK - S
    kp = jnp.pad(k, ((0, 0), (0, pad), (0, 0), (0, 0)))
    vp = jnp.pad(v, ((0, 0), (0, pad), (0, 0), (0, 0)))
    kb = kp.reshape(B, nb, MOBA_BLOCK, H, D)
    vb = vp.reshape(B, nb, MOBA_BLOCK, H, D)
    k_mean = jnp.mean(kb.astype(F32), axis=2).astype(q.dtype)
    n_sel = min(MOBA_TOPK, nb - 1)
    bi = jnp.arange(B)[:, None, None, None]
    hi = jnp.arange(H)[None, None, :, None]

    def chunk(i):
        c0 = i * MOBA_Q_CHUNK
        blk = c0 // MOBA_BLOCK
        b0 = blk * MOBA_BLOCK
        qc = lax.dynamic_slice_in_dim(q, c0, MOBA_Q_CHUNK, axis=1)
        q_pos = c0 + jnp.arange(MOBA_Q_CHUNK)
        k_own = lax.dynamic_slice_in_dim(kp, b0, MOBA_BLOCK, axis=1)
        v_own = lax.dynamic_slice_in_dim(vp, b0, MOBA_BLOCK, axis=1)
        own_mask = (b0 + jnp.arange(MOBA_BLOCK))[None, :] <= q_pos[:, None]
        if n_sel == 0:
            return moba_core(qc, None, None, None, k_own, v_own, own_mask)
        gs = jnp.einsum('bqhd,bnhd->bqhn', qc, k_mean).astype(F32)
        gs = jnp.where(jnp.arange(nb) < blk, gs, NEG)
        _, ti = lax.top_k(gs, n_sel)
        k_sel = kb[bi, ti, :, hi].reshape(B, MOBA_Q_CHUNK, H, n_sel * MOBA_BLOCK, D)
        v_sel = vb[bi, ti, :, hi].reshape(B, MOBA_Q_CHUNK, H, n_sel * MOBA_BLOCK, D)
        sel_mask = jnp.repeat(ti < blk, MOBA_BLOCK, axis=-1)
        return moba_core(qc, k_sel, v_sel, sel_mask, k_own, v_own, own_mask)

    o = lax.map(chunk, jnp.arange(S // MOBA_Q_CHUNK))
    return jnp.moveaxis(o, 0, 1).reshape(B, S, H, D)


def moba_sample(q, k_new, v_new, cache_k, cache_v, page_table):
    B, T, H, D = q.shape
    ppb = MOBA_BLOCK // PAGE_SIZE
    n_full = PAST_LEN // MOBA_BLOCK
    r = PAST_LEN - n_full * MOBA_BLOCK
    own_pages = page_table[:, n_full * ppb:]
    k_own = jnp.concatenate([cache_k[own_pages].reshape(B, r, H, D), k_new], axis=1)
    v_own = jnp.concatenate([cache_v[own_pages].reshape(B, r, H, D), v_new], axis=1)
    n_sel = min(MOBA_TOPK, n_full)
    if n_sel > 0:
        blk_pages = page_table[:, :n_full * ppb]
        page_sum = jnp.sum(cache_k, axis=1, dtype=F32)
        k_mean = (page_sum[blk_pages].reshape(B, n_full, ppb, H, D).sum(2) / MOBA_BLOCK).astype(q.dtype)
        gs = jnp.einsum('bthd,bnhd->bthn', q, k_mean).astype(F32)
        _, ti = lax.top_k(gs, n_sel)
        blk_phys = blk_pages.reshape(B, n_full, ppb)
    else:
        ti = jnp.zeros((B, T, H, 0), jnp.int32)
    bi = jnp.arange(B)[:, None, None]
    hi = jnp.arange(H)[None, :, None, None]

    def step(args):
        t, q_t, ti_t = args
        own_mask = (jnp.arange(r + T) < r + t + 1)[None, :]
        qq = q_t[:, None]
        if n_sel == 0:
            return moba_core(qq, None, None, None, k_own, v_own, own_mask)
        phys = blk_phys[bi, ti_t]
        k_sel = cache_k[phys, :, hi].reshape(B, 1, H, n_sel * MOBA_BLOCK, D)
        v_sel = cache_v[phys, :, hi].reshape(B, 1, H, n_sel * MOBA_BLOCK, D)
        sel_mask = jnp.ones((B, 1, H, n_sel * MOBA_BLOCK), bool)
        return moba_core(qq, k_sel, v_sel, sel_mask, k_own, v_own, own_mask)

    o = lax.map(step, (jnp.arange(T), jnp.moveaxis(q, 1, 0), jnp.moveaxis(ti, 1, 0)))
    return jnp.moveaxis(o[:, :, 0], 0, 1)


def peer_ffn(h, w_pq, sub_keys, u_tab, v_tab):
    n = h.shape[0]
    nc = -(-n // PEER_CHUNK)
    hp = jnp.pad(h, ((0, nc * PEER_CHUNK - n), (0, 0))).reshape(nc, PEER_CHUNK, D_MODEL)

    def chunk(hc):
        q = (hc @ w_pq).reshape(PEER_CHUNK, PEER_HEADS, 2, PEER_DKEY // 2)
        s = jnp.einsum('nhpd,hpkd->nhpk', q, sub_keys).astype(F32)
        s1, i1 = lax.top_k(s[:, :, 0], PEER_TOPK)
        s2, i2 = lax.top_k(s[:, :, 1], PEER_TOPK)
        cand_s = (s1[..., :, None] + s2[..., None, :]).reshape(PEER_CHUNK, PEER_HEADS, PEER_TOPK * PEER_TOPK)
        cand_i = (i1[..., :, None] * PEER_NKEYS + i2[..., None, :]).reshape(PEER_CHUNK, PEER_HEADS, PEER_TOPK * PEER_TOPK)
        top_s, top_j = lax.top_k(cand_s, PEER_TOPK)
        idx = jnp.take_along_axis(cand_i, top_j, axis=-1)
        g = jax.nn.softmax(top_s, axis=-1).astype(hc.dtype)
        a = jax.nn.gelu(jnp.einsum('nd,nhkd->nhk', hc, u_tab[idx]))
        return jnp.einsum('nhk,nhkd->nd', g * a, v_tab[idx])

    return lax.map(chunk, hp).reshape(nc * PEER_CHUNK, D_MODEL)[:n]


def layer_forward(x, c, pos, attend, w_ada, b_ada, g_mix, w_in, g_q, w_uq, g_kv, w_uk, w_uv,
                  beta_mla, beta_moba, w_o, g_ffn, w_pq, sub_keys, u_tab, v_tab):
    B, S, _ = x.shape
    mod = jax.nn.silu(c) @ w_ada + b_ada
    sh1, sc1, gt1, sh2, sc2, gt2 = jnp.split(mod[:, None, :], 6, axis=-1)
    h = rms_norm(x, g_mix) * (1.0 + sc1) + sh1
    q_lat, q_rope, ckv, krope, qm, km, vm = project_inputs(h, pos, w_in, g_q, w_uq, g_kv, w_uk)
    o_lat, o_moba = attend(q_lat, q_rope, ckv, krope, qm, km, vm)
    o_mla = jnp.einsum('bshr,rhv->bshv', o_lat, w_uv).reshape(B, S, MLA_WIDTH)
    o = jnp.concatenate([rms_norm(o_mla, beta_mla),
                         rms_norm(o_moba.reshape(B, S, MOBA_WIDTH), beta_moba)], axis=-1)
    x = x + gt1 * (o @ w_o)
    h = rms_norm(x, g_ffn) * (1.0 + sc2) + sh2
    x = x + gt2 * peer_ffn(h.reshape(B * S, D_MODEL), w_pq, sub_keys, u_tab, v_tab).reshape(B, S, D_MODEL)
    return x, (ckv, krope, km, vm)


def setup_inputs(seed: int = 0) -> dict:
    key = jax.random.key(seed)
    ks = iter(jax.random.split(key, 40))

    def nrm(shape, s):
        return jax.random.normal(next(ks), shape, F32) * s

    def gain(shape):
        return 1.0 + nrm(shape, 0.02)

    n_pages = PAST_LEN // PAGE_SIZE
    n_used = DEC_BATCH * n_pages
    n_pool = n_used + n_used // 4
    page_table = jax.random.permutation(next(ks), n_pool)[:n_used].reshape(DEC_BATCH, n_pages).astype(jnp.int32)
    L = DEPTH
    return {
        'x_prompt': nrm((BATCH, SEQ, D_MODEL), 1.0),
        'x_sample': nrm((DEC_BATCH, DEC_SEQ, D_MODEL), 1.0),
        'cache_mla_ckv': nrm((L, n_pool, PAGE_SIZE, MLA_KV_RANK), 1.0),
        'cache_mla_krope': nrm((L, n_pool, PAGE_SIZE, MLA_ROPE), 1.0),
        'cache_moba_k': nrm((L, n_pool, PAGE_SIZE, MOBA_HEADS, MOBA_HD), 1.0),
        'cache_moba_v': nrm((L, n_pool, PAGE_SIZE, MOBA_HEADS, MOBA_HD), 1.0),
        'page_table': page_table,
        'c_prompt': nrm((BATCH, D_MODEL), 1.0),
        'c_sample': nrm((DEC_BATCH, D_MODEL), 1.0),
        'w_ada': nrm((L, D_MODEL, 6 * D_MODEL), 0.5 * D_MODEL ** -0.5),
        'b_ada': nrm((L, 6 * D_MODEL), 0.02),
        'g_mix': gain((L, D_MODEL)),
        'w_in': nrm((L, D_MODEL, IN_WIDTH), D_MODEL ** -0.5),
        'g_q': gain((L, MLA_Q_RANK)),
        'w_uq': nrm((L, MLA_Q_RANK, MLA_HEADS * (MLA_NOPE + MLA_ROPE)), MLA_Q_RANK ** -0.5),
        'g_kv': gain((L, MLA_KV_RANK)),
        'w_uk': nrm((L, MLA_KV_RANK, MLA_HEADS, MLA_NOPE), MLA_KV_RANK ** -0.5),
        'w_uv': nrm((L, MLA_KV_RANK, MLA_HEADS, MLA_V), MLA_KV_RANK ** -0.5),
        'beta_mla': gain((L, MLA_WIDTH)),
        'beta_moba': gain((L, MOBA_WIDTH)),
        'w_o': nrm((L, MIX_WIDTH, D_MODEL), MIX_WIDTH ** -0.5),
        'g_ffn': gain((L, D_MODEL)),
        'w_pq': nrm((L, D_MODEL, PEER_HEADS * PEER_DKEY), D_MODEL ** -0.5),
        'sub_keys': nrm((L, PEER_HEADS, 2, PEER_NKEYS, PEER_DKEY // 2), (PEER_DKEY // 2) ** -0.5),
        'u_tab': nrm((L, PEER_N, D_MODEL), D_MODEL ** -0.5),
        'v_tab': nrm((L, PEER_N, D_MODEL), 0.1),
        'g_final': gain((D_MODEL,)),
    }


def reference(x_prompt, x_sample, cache_mla_ckv, cache_mla_krope, cache_moba_k, cache_moba_v, page_table,
              c_prompt, c_sample, w_ada, b_ada, g_mix, w_in, g_q, w_uq, g_kv, w_uk, w_uv, beta_mla, beta_moba,
              w_o, g_ffn, w_pq, sub_keys, u_tab, v_tab, g_final):
    pos_p = jnp.arange(x_prompt.shape[1], dtype=F32)
    pos_s = PAST_LEN + jnp.arange(x_sample.shape[1], dtype=F32)

    def attend_prompt(q_lat, q_rope, ckv, krope, qm, km, vm):
        return mla_prompt(q_lat, q_rope, ckv, krope), moba_prompt(qm, km, vm)

    xp, xs = x_prompt, x_sample
    new_p, new_s = [], []
    for l in range(DEPTH):
        weights = (w_ada[l], b_ada[l], g_mix[l], w_in[l], g_q[l], w_uq[l], g_kv[l], w_uk[l], w_uv[l],
                   beta_mla[l], beta_moba[l], w_o[l], g_ffn[l], w_pq[l], sub_keys[l], u_tab[l], v_tab[l])

        def attend_sample(q_lat, q_rope, ckv, krope, qm, km, vm, l=l):
            return (mla_sample(q_lat, q_rope, ckv, krope, cache_mla_ckv[l], cache_mla_krope[l], page_table),
                    moba_sample(qm, km, vm, cache_moba_k[l], cache_moba_v[l], page_table))

        xp, st_p = layer_forward(xp, c_prompt, pos_p, attend_prompt, *weights)
        xs, st_s = layer_forward(xs, c_sample, pos_s, attend_sample, *weights)
        new_p.append(st_p)
        new_s.append(st_s)

    def stack(sts, i):
        return jnp.stack([s[i] for s in sts])

    y_prompt = rms_norm(xp, g_final)
    y_sample = rms_norm(xs, g_final)
    return (y_prompt, y_sample,
            stack(new_p, 0), stack(new_p, 1), stack(new_p, 2), stack(new_p, 3),
            stack(new_s, 0), stack(new_s, 1), stack(new_s, 2), stack(new_s, 3))
```

```python
import functools

import jax
import jax.numpy as jnp
from jax import lax
from jax.experimental import pallas as pl
from jax.experimental.pallas import tpu as pltpu

F32 = jnp.float32
BF16 = jnp.bfloat16
I32 = jnp.int32

EPS = 1e-6
NEG = -1e30
ROPE_THETA = 500000.0
PAGE_SIZE = 128
MLA_HEADS = 8
MLA_NOPE = 64
MLA_ROPE = 32
MLA_V = 64
MLA_Q_RANK = 256
MLA_KV_RANK = 128
MLA_QK = MLA_KV_RANK + MLA_ROPE
MOBA_HEADS = 8
MOBA_HD = 64
MOBA_ROT = MOBA_HD // 4
MOBA_BLOCK = 256
MOBA_TOPK = 3
MOBA_WIDTH = MOBA_HEADS * MOBA_HD
MLA_WIDTH = MLA_HEADS * MLA_V
PEER_HEADS = 8
PEER_NKEYS = 128
PEER_DKEY = 256
PEER_TOPK = 16
PEER_SEL = PEER_HEADS * PEER_TOPK

LANES = 128
V7X_VMEM_BYTES = 64 * 1024 * 1024
VMEM_LIMIT = V7X_VMEM_BYTES * 7 // 8
GATE_PITCH = PEER_NKEYS + 8

_C_CQ = 0
_C_CKV = _C_CQ + MLA_Q_RANK
_C_QM = _C_CKV + MLA_KV_RANK
_C_KM = _C_QM + MOBA_WIDTH
_C_VM = _C_KM + MOBA_WIDTH
_C_KR = _C_VM + MOBA_WIDTH
_C_END = _C_KR + LANES


def _cparams(sem):
    return pltpu.CompilerParams(dimension_semantics=sem, vmem_limit_bytes=VMEM_LIMIT)


def _rms(x):
    return x * lax.rsqrt(jnp.mean(x * x, axis=-1, keepdims=True) + EPS)


def _rope(x, tab_ref, half):
    n = x.shape[-1]
    c, s1, s2 = tab_ref[0], tab_ref[1], tab_ref[2]
    return x * c + pltpu.roll(x, n - half, 1) * s1 + pltpu.roll(x, half, 1) * s2


def _ada_kernel(c_ref, w_ref, b_ref, o_ref):
    c = c_ref[...]
    a = (c * jax.nn.sigmoid(c)).astype(BF16)
    o_ref[...] = jnp.dot(a, w_ref[...].astype(BF16), preferred_element_type=F32) + b_ref[...]


def _ada(c, w, b):
    n, d = c.shape
    width = w.shape[1]
    tn = width // 4
    return pl.pallas_call(
        _ada_kernel,
        grid=(width // tn,),
        in_specs=[pl.BlockSpec((n, d), lambda j: (0, 0)),
                  pl.BlockSpec((d, tn), lambda j: (0, j)),
                  pl.BlockSpec((1, tn), lambda j: (0, j))],
        out_specs=pl.BlockSpec((n, tn), lambda j: (0, j)),
        out_shape=jax.ShapeDtypeStruct((n, width), F32),
        compiler_params=_cparams(("arbitrary",)),
        name="ada",
    )(c, w, b.reshape(1, width))


def _proj_kernel(n_mean, x_ref, mod_ref, gmix_ref, win_ref, gq_ref, wuq_ref, gkv_ref, wuk_ref,
                 tmla_ref, tkr_ref, tmob_ref,
                 qcat_ref, kcat_ref, ckv_ref, krope_ref, qm_ref, kmb_ref, vmb_ref, km_ref, vm_ref,
                 *maybe_kmean_ref):
    bb, ts, d = x_ref.shape
    m = bb * ts
    mod = mod_ref[...]
    h = _rms(x_ref[...]) * gmix_ref[...] * (1.0 + mod[:, 1:2, :]) + mod[:, 0:1, :]
    z = jnp.dot(h.reshape(m, d).astype(BF16), win_ref[...], preferred_element_type=F32)

    def rope(x, ref, half):
        if bb == 1:
            return _rope(x, ref, half)
        w = x.shape[-1]
        x3 = x.reshape(bb, ts, w)
        r1 = pltpu.roll(x, w - half, 1).reshape(bb, ts, w)
        r2 = pltpu.roll(x, half, 1).reshape(bb, ts, w)
        return (x3 * ref[0][None] + r1 * ref[1][None] + r2 * ref[2][None]).reshape(m, w)

    cq = _rms(z[:, _C_CQ:_C_CQ + MLA_Q_RANK]) * gq_ref[...]
    q = jnp.dot(cq.astype(BF16), wuq_ref[...], preferred_element_type=F32)
    scale = (MLA_NOPE + MLA_ROPE) ** -0.5
    nope_w = MLA_HEADS * MLA_NOPE
    q_rope = rope(q[:, nope_w:], tmla_ref, MLA_ROPE // 2) * scale
    for hh in range(MLA_HEADS):
        qn = q[:, hh * MLA_NOPE:(hh + 1) * MLA_NOPE].astype(BF16)
        ql = jnp.dot(qn, wuk_ref[hh], preferred_element_type=F32) * scale
        qcat_ref[:, hh, :, 0:MLA_KV_RANK] = ql.reshape(bb, ts, MLA_KV_RANK).astype(BF16)
        qr = q_rope[:, hh * MLA_ROPE:(hh + 1) * MLA_ROPE]
        qcat_ref[:, hh, :, MLA_KV_RANK:MLA_QK] = qr.reshape(bb, ts, MLA_ROPE).astype(BF16)

    ckv = _rms(z[:, _C_CKV:_C_CKV + MLA_KV_RANK]) * gkv_ref[...]
    kr = rope(z[:, _C_KR:_C_KR + LANES], tkr_ref, MLA_ROPE // 2)[:, 0:MLA_ROPE]
    ckv_ref[...] = ckv.reshape(bb, ts, MLA_KV_RANK)
    krope_ref[...] = kr.reshape(bb, ts, MLA_ROPE)
    kcat_ref[:, :, 0:MLA_KV_RANK] = ckv.reshape(bb, ts, MLA_KV_RANK).astype(BF16)
    kcat_ref[:, :, MLA_KV_RANK:MLA_QK] = kr.reshape(bb, ts, MLA_ROPE).astype(BF16)

    qm = rope(z[:, _C_QM:_C_QM + MOBA_WIDTH], tmob_ref, MOBA_ROT // 2) * (MOBA_HD ** -0.5)
    km = rope(z[:, _C_KM:_C_KM + MOBA_WIDTH], tmob_ref, MOBA_ROT // 2)
    vm = z[:, _C_VM:_C_VM + MOBA_WIDTH]
    qm_ref[...] = qm.reshape(bb, ts, MOBA_WIDTH).astype(BF16)
    kmb_ref[...] = km.reshape(bb, ts, MOBA_WIDTH).astype(BF16)
    vmb_ref[...] = vm.reshape(bb, ts, MOBA_WIDTH).astype(BF16)
    km_ref[...] = km.reshape(bb, ts, MOBA_WIDTH)
    vm_ref[...] = vm.reshape(bb, ts, MOBA_WIDTH)
    if n_mean:
        kmean_ref, = maybe_kmean_ref
        for i in range(n_mean):
            blk = km[i * MOBA_BLOCK:(i + 1) * MOBA_BLOCK]
            kmean_ref[0, 0, i:i + 1, :] = jnp.sum(blk, axis=0, keepdims=True) * (1.0 / MOBA_BLOCK)


def _proj(x, mod, pos, w, bb, ts, with_kmean):
    b, s, d = x.shape
    nb, nt = b // bb, s // ts
    n_mean = ts // MOBA_BLOCK if with_kmean else 0
    tmla = _rope_tables(pos, MLA_ROPE, MLA_ROPE, MLA_HEADS)
    tkr = _rope_tables(pos, MLA_ROPE, LANES, 1)
    tmob = _rope_tables(pos, MOBA_ROT, MOBA_HD, MOBA_HEADS)

    def full(a):
        return pl.BlockSpec(a.shape, lambda i, j: (0,) * a.ndim)

    def tok(wd):
        return pl.BlockSpec((bb, ts, wd), lambda i, j: (i, j, 0))

    def tab(wd):
        return pl.BlockSpec((3, ts, wd), lambda i, j: (0, j, 0))

    out_shape = [jax.ShapeDtypeStruct((b, MLA_HEADS, s, MLA_QK), BF16),
                 jax.ShapeDtypeStruct((b, s, MLA_QK), BF16),
                 jax.ShapeDtypeStruct((b, s, MLA_KV_RANK), F32),
                 jax.ShapeDtypeStruct((b, s, MLA_ROPE), F32),
                 jax.ShapeDtypeStruct((b, s, MOBA_WIDTH), BF16),
                 jax.ShapeDtypeStruct((b, s, MOBA_WIDTH), BF16),
                 jax.ShapeDtypeStruct((b, s, MOBA_WIDTH), BF16),
                 jax.ShapeDtypeStruct((b, s, MOBA_WIDTH), F32),
                 jax.ShapeDtypeStruct((b, s, MOBA_WIDTH), F32)]
    out_specs = [pl.BlockSpec((bb, MLA_HEADS, ts, MLA_QK), lambda i, j: (i, 0, j, 0)),
                 tok(MLA_QK), tok(MLA_KV_RANK), tok(MLA_ROPE),
                 tok(MOBA_WIDTH), tok(MOBA_WIDTH), tok(MOBA_WIDTH), tok(MOBA_WIDTH), tok(MOBA_WIDTH)]
    if n_mean:
        out_shape.append(jax.ShapeDtypeStruct((b, nt, n_mean, MOBA_WIDTH), F32))
        out_specs.append(pl.BlockSpec((1, 1, n_mean, MOBA_WIDTH), lambda i, j: (i, j, 0, 0)))
    args = (x, mod, w["g_mix"], w["w_in"], w["g_q"], w["w_uq"], w["g_kv"], w["w_uk_t"], tmla, tkr, tmob)
    in_specs = [tok(d), pl.BlockSpec((bb, 6, d), lambda i, j: (i, 0, 0)),
                full(w["g_mix"]), full(w["w_in"]), full(w["g_q"]), full(w["w_uq"]), full(w["g_kv"]),
                full(w["w_uk_t"]), tab(MLA_HEADS * MLA_ROPE), tab(LANES), tab(MOBA_WIDTH)]
    return pl.pallas_call(
        functools.partial(_proj_kernel, n_mean),
        grid=(nb, nt), in_specs=in_specs, out_specs=out_specs, out_shape=out_shape,
        compiler_params=_cparams(("arbitrary", "arbitrary")),
        name="proj",
    )(*args)


def _rope_tables(pos, rot, unit, rep):
    half = rot // 2
    inv = ROPE_THETA ** (-jnp.arange(half, dtype=F32) / half)
    ang = pos[:, None] * inv[None, :]
    cos, sin = jnp.cos(ang), jnp.sin(ang)
    n = pos.shape[0]
    one, zero, zh = jnp.ones((n, unit - rot), F32), jnp.zeros((n, unit - rot), F32), jnp.zeros((n, half), F32)
    c = jnp.concatenate([cos, cos, one], axis=1)
    s1 = jnp.concatenate([-sin, zh, zero], axis=1)
    s2 = jnp.concatenate([zh, sin, zero], axis=1)
    return jnp.stack([jnp.tile(t, (1, rep)) for t in (c, s1, s2)])


def _mla_kernel(tq, tk, q_ref, k_ref, wuv_ref, o_ref):
    i = pl.program_id(1)
    rows = MLA_HEADS * tq
    q = q_ref[0].reshape(rows, MLA_QK)
    q0 = i * tq

    def step(j, carry, masked):
        m, l, acc = carry
        kb = k_ref[0, pl.ds(pl.multiple_of(j * tk, tk), tk), :]
        s = lax.dot_general(q, kb, (((1,), (1,)), ((), ())), preferred_element_type=F32)
        if masked:
            qpos = q0 + lax.broadcasted_iota(I32, (MLA_HEADS, tq, tk), 1).reshape(rows, tk)
            kpos = j * tk + lax.broadcasted_iota(I32, (rows, tk), 1)
            s = jnp.where(kpos <= qpos, s, NEG)
        m_new = jnp.maximum(m, jnp.max(s, axis=-1, keepdims=True))
        alpha = jnp.exp(m - m_new)
        p = jnp.exp(s - m_new)
        l = alpha * l + jnp.sum(p, axis=-1, keepdims=True)
        acc = alpha * acc + jnp.dot(p.astype(BF16), kb[:, 0:MLA_KV_RANK], preferred_element_type=F32)
        return m_new, l, acc

    init = (jnp.full((rows, 1), NEG, F32), jnp.zeros((rows, 1), F32), jnp.zeros((rows, MLA_KV_RANK), F32))
    nfull = q0 // tk
    carry = lax.fori_loop(0, nfull, lambda j, c: step(j, c, False), init)
    m, l, acc = step(nfull, carry, True)
    o_lat = (acc / l).astype(BF16)
    out = jnp.zeros((tq, MLA_WIDTH), F32)
    for hh in range(MLA_HEADS):
        out = out + jnp.dot(o_lat[hh * tq:(hh + 1) * tq], wuv_ref[hh], preferred_element_type=F32)
    o_ref[0] = out


def _mla_prompt(qcat, kcat, wuv_pad, tq, tk):
    b, _, s, _ = qcat.shape
    return pl.pallas_call(
        functools.partial(_mla_kernel, tq, tk),
        grid=(b, s // tq),
        in_specs=[pl.BlockSpec((1, MLA_HEADS, tq, MLA_QK), lambda i, j: (i, 0, j, 0)),
                  pl.BlockSpec((1, s, MLA_QK), lambda i, j: (i, 0, 0)),
                  pl.BlockSpec(wuv_pad.shape, lambda i, j: (0, 0, 0))],
        out_specs=pl.BlockSpec((1, tq, MLA_WIDTH), lambda i, j: (i, j, 0)),
        out_shape=jax.ShapeDtypeStruct((b, s, MLA_WIDTH), F32),
        compiler_params=_cparams(("arbitrary", "arbitrary")),
        name="mla",
    )(qcat, kcat, wuv_pad)


def _moba_kernel(nb, q_ref, k_ref, v_ref, kmean_ref, o_ref):
    i = pl.program_id(1)
    tq = MOBA_BLOCK
    col = lax.broadcasted_iota(I32, (tq, nb), 1)
    qpos = lax.broadcasted_iota(I32, (tq, tq), 0)
    kpos = lax.broadcasted_iota(I32, (tq, tq), 1)
    for hh in range(MOBA_HEADS):
        lo = hh * MOBA_HD
        q = q_ref[0, :, lo:lo + MOBA_HD]
        kmean = kmean_ref[0, :, lo:lo + MOBA_HD].astype(BF16)
        gs = lax.dot_general(q, kmean, (((1,), (1,)), ((), ())), preferred_element_type=F32)
        gs = jnp.where(col < i, gs, NEG)
        cnt = jnp.zeros((tq, nb), F32)
        for jj in range(nb):
            cj = gs[:, jj:jj + 1]
            cnt = cnt + jnp.where(cj > gs, 1.0, 0.0) + jnp.where((cj == gs) & (jj < col), 1.0, 0.0)
        sel = jnp.where((cnt < MOBA_TOPK) & (col < i), 1.0, 0.0)

        def step(j, carry):
            m, l, acc = carry
            off = pl.multiple_of(j * tq, tq)
            kb = k_ref[0, pl.ds(off, tq), lo:lo + MOBA_HD]
            vb = v_ref[0, pl.ds(off, tq), lo:lo + MOBA_HD]
            s = lax.dot_general(q, kb, (((1,), (1,)), ((), ())), preferred_element_type=F32)
            selcol = jnp.sum(jnp.where(col == j, sel, 0.0), axis=-1, keepdims=True)
            selcol = jnp.where(j == i, 1.0, selcol)
            s = jnp.where(kpos + (j - i) * tq <= qpos, s, NEG)
            s = jnp.where(selcol > 0.5, s, NEG)
            m_new = jnp.maximum(m, jnp.max(s, axis=-1, keepdims=True))
            alpha = jnp.exp(m - m_new)
            p = jnp.where(s > 0.5 * NEG, jnp.exp(s - m_new), 0.0)
            l = alpha * l + jnp.sum(p, axis=-1, keepdims=True)
            acc = alpha * acc + jnp.dot(p.astype(BF16), vb, preferred_element_type=F32)
            return m_new, l, acc

        init = (jnp.full((tq, 1), NEG, F32), jnp.zeros((tq, 1), F32), jnp.zeros((tq, MOBA_HD), F32))
        m, l, acc = lax.fori_loop(0, i + 1, step, init)
        o_ref[0, :, lo:lo + MOBA_HD] = acc / l


def _moba_prompt(qm, kmb, vmb, kmean):
    b, s, _ = qm.shape
    nb = s // MOBA_BLOCK
    return pl.pallas_call(
        functools.partial(_moba_kernel, nb),
        grid=(b, nb),
        in_specs=[pl.BlockSpec((1, MOBA_BLOCK, MOBA_WIDTH), lambda i, j: (i, j, 0)),
                  pl.BlockSpec((1, s, MOBA_WIDTH), lambda i, j: (i, 0, 0)),
                  pl.BlockSpec((1, s, MOBA_WIDTH), lambda i, j: (i, 0, 0)),
                  pl.BlockSpec((1, nb, MOBA_WIDTH), lambda i, j: (i, 0, 0))],
        out_specs=pl.BlockSpec((1, MOBA_BLOCK, MOBA_WIDTH), lambda i, j: (i, j, 0)),
        out_shape=jax.ShapeDtypeStruct((b, s, MOBA_WIDTH), F32),
        compiler_params=_cparams(("arbitrary", "arbitrary")),
        name="moba",
    )(qm, kmb, vmb, kmean)


def _mla_dec_kernel(pps, t_new, pt_ref, q_ref, knew_ref, wuv_ref, *rest):
    ckv_refs, kr_refs = rest[:pps], rest[pps:2 * pps]
    o_ref, m_ref, l_ref, acc_ref = rest[2 * pps:]
    c = pl.program_id(1)
    rows = MLA_HEADS * t_new
    q = q_ref[0].reshape(rows, MLA_QK)
    ql, qr = q[:, 0:MLA_KV_RANK], q[:, MLA_KV_RANK:MLA_QK]

    @pl.when(c == 0)
    def _():
        m_ref[...] = jnp.full(m_ref.shape, NEG, F32)
        l_ref[...] = jnp.zeros(l_ref.shape, F32)
        acc_ref[...] = jnp.zeros(acc_ref.shape, F32)

    nt = (((1,), (1,)), ((), ()))
    vs, ss = [], []
    for k in range(pps):
        ckv = ckv_refs[k][0].astype(BF16)
        kr = kr_refs[k][0].astype(BF16)
        ss.append(lax.dot_general(ql, ckv, nt, preferred_element_type=F32)
                  + lax.dot_general(qr, kr, nt, preferred_element_type=F32))
        vs.append(ckv)
    s = jnp.concatenate(ss, axis=1)
    m_old = m_ref[...]
    m_new = jnp.maximum(m_old, jnp.max(s, axis=-1, keepdims=True))
    alpha = jnp.exp(m_old - m_new)
    p = jnp.exp(s - m_new).astype(BF16)
    pv = jnp.zeros((rows, MLA_KV_RANK), F32)
    for k in range(pps):
        pv = pv + jnp.dot(p[:, k * PAGE_SIZE:(k + 1) * PAGE_SIZE], vs[k], preferred_element_type=F32)
    l_ref[...] = alpha * l_ref[...] + jnp.sum(p.astype(F32), axis=-1, keepdims=True)
    acc_ref[...] = alpha * acc_ref[...] + pv
    m_ref[...] = m_new

    @pl.when(c == pl.num_programs(1) - 1)
    def _():
        kn = knew_ref[0]
        sn = lax.dot_general(q, kn, nt, preferred_element_type=F32)
        tq = lax.broadcasted_iota(I32, (MLA_HEADS, t_new, t_new), 1).reshape(rows, t_new)
        tk = lax.broadcasted_iota(I32, (rows, t_new), 1)
        sn = jnp.where(tk <= tq, sn, NEG)
        m1 = m_ref[...]
        m2 = jnp.maximum(m1, jnp.max(sn, axis=-1, keepdims=True))
        a2 = jnp.exp(m1 - m2)
        pn = jnp.exp(sn - m2)
        l2 = a2 * l_ref[...] + jnp.sum(pn, axis=-1, keepdims=True)
        acc2 = a2 * acc_ref[...] + jnp.dot(pn.astype(BF16), kn[:, 0:MLA_KV_RANK], preferred_element_type=F32)
        o_lat = (acc2 / l2).astype(BF16)
        out = jnp.zeros((t_new, MLA_WIDTH), F32)
        for hh in range(MLA_HEADS):
            out = out + jnp.dot(o_lat[hh * t_new:(hh + 1) * t_new], wuv_ref[hh], preferred_element_type=F32)
        o_ref[0] = out


def _mla_sample(qcat, kcat_new, wuv_pad, cache_ckv, cache_kr, page_table, pps):
    b, _, t_new, _ = qcat.shape
    n_pages = page_table.shape[1]
    rows = MLA_HEADS * t_new

    def page_spec(k, width):
        return pl.BlockSpec((1, PAGE_SIZE, width), lambda i, c, pt: (pt[i, c * pps + k], 0, 0))

    in_specs = ([pl.BlockSpec((1, MLA_HEADS, t_new, MLA_QK), lambda i, c, pt: (i, 0, 0, 0)),
                 pl.BlockSpec((1, t_new, MLA_QK), lambda i, c, pt: (i, 0, 0)),
                 pl.BlockSpec(wuv_pad.shape, lambda i, c, pt: (0, 0, 0))]
                + [page_spec(k, MLA_KV_RANK) for k in range(pps)]
                + [page_spec(k, MLA_ROPE) for k in range(pps)])
    grid_spec = pltpu.PrefetchScalarGridSpec(
        num_scalar_prefetch=1, grid=(b, n_pages // pps), in_specs=in_specs,
        out_specs=pl.BlockSpec((1, t_new, MLA_WIDTH), lambda i, c, pt: (i, 0, 0)),
        scratch_shapes=[pltpu.VMEM((rows, 1), F32), pltpu.VMEM((rows, 1), F32),
                        pltpu.VMEM((rows, MLA_KV_RANK), F32)])
    return pl.pallas_call(
        functools.partial(_mla_dec_kernel, pps, t_new),
        grid_spec=grid_spec,
        out_shape=jax.ShapeDtypeStruct((b, t_new, MLA_WIDTH), F32),
        compiler_params=_cparams(("arbitrary", "arbitrary")),
        name="mla_dec",
    )(page_table, qcat, kcat_new, wuv_pad, *([cache_ckv] * pps), *([cache_kr] * pps))


def _moba_dec_kernel(pps, t_new, n_pages, pt_ref, q_ref, knew_ref, vnew_ref, *rest):
    k_refs, v_refs = rest[:pps], rest[pps:2 * pps]
    o_ref, qbd_ref, s_ref, ksum_ref, l_ref, own_ref, acc_ref = rest[2 * pps:]
    phase, c = pl.program_id(1), pl.program_id(2)
    nc = pl.num_programs(2)
    rows = MOBA_HEADS * t_new
    n_blk = n_pages * PAGE_SIZE // MOBA_BLOCK
    ppb = MOBA_BLOCK // PAGE_SIZE
    nt = (((1,), (1,)), ((), ()))

    @pl.when((phase == 0) & (c == 0))
    def _():
        qt = jnp.concatenate([q_ref[0].astype(F32)] * MOBA_HEADS, axis=0)
        rh = lax.broadcasted_iota(I32, (MOBA_HEADS, t_new, MOBA_WIDTH), 0).reshape(rows, MOBA_WIDTH)
        ch = lax.broadcasted_iota(I32, (rows, MOBA_HEADS, MOBA_HD), 1).reshape(rows, MOBA_WIDTH)
        qbd_ref[...] = jnp.where(rh == ch, qt, 0.0).astype(BF16)

    @pl.when(phase == 0)
    def _():
        qbd = qbd_ref[...]
        for k in range(0, pps, ppb):
            ks = jnp.zeros((1, MOBA_WIDTH), F32)
            for kk in range(k, k + ppb):
                page = k_refs[kk][0]
                ks = ks + jnp.sum(page, axis=0, keepdims=True)
                s_ref[c * pps + kk] = lax.dot_general(qbd, page.astype(BF16), nt, preferred_element_type=F32)
            ksum_ref[pl.ds(c * (pps // ppb) + k // ppb, 1), :] = ks

    @pl.when((phase == 0) & (c == nc - 1))
    def _():
        qbd = qbd_ref[...]
        kmean = (ksum_ref[...] * (1.0 / MOBA_BLOCK)).astype(BF16)
        gs = lax.dot_general(qbd, kmean, nt, preferred_element_type=F32)
        bcol = lax.broadcasted_iota(I32, (rows, n_blk), 1)
        sel = jnp.zeros((rows, n_blk), F32)
        for _ in range(min(MOBA_TOPK, n_blk)):
            mx = jnp.max(gs, axis=-1, keepdims=True)
            first = jnp.min(jnp.where(gs == mx, bcol, n_blk), axis=-1, keepdims=True)
            hit = bcol == first
            sel = jnp.where(hit, 1.0, sel)
            gs = jnp.where(hit, -jnp.inf, gs)
        kn = knew_ref[0].astype(BF16)
        so = lax.dot_general(qbd, kn, nt, preferred_element_type=F32)
        tq = lax.broadcasted_iota(I32, (MOBA_HEADS, t_new, t_new), 1).reshape(rows, t_new)
        tk = lax.broadcasted_iota(I32, (rows, t_new), 1)
        so = jnp.where(tk <= tq, so, NEG)

        def selcol(pg):
            return jnp.sum(jnp.where(bcol == pg // ppb, sel, 0.0), axis=-1, keepdims=True) > 0.5

        def max_body(pg, m):
            sp = jnp.where(selcol(pg), s_ref[pg], NEG)
            return jnp.maximum(m, jnp.max(sp, axis=-1, keepdims=True))

        m = lax.fori_loop(0, n_pages, max_body, jnp.max(so, axis=-1, keepdims=True))

        def exp_body(pg, l):
            keep = selcol(pg)
            p = jnp.where(keep, jnp.exp(jnp.where(keep, s_ref[pg], NEG) - m), 0.0)
            s_ref[pg] = p
            return l + jnp.sum(p, axis=-1, keepdims=True)

        po = jnp.exp(so - m)
        l_ref[...] = lax.fori_loop(0, n_pages, exp_body, jnp.sum(po, axis=-1, keepdims=True))
        own_ref[...] = jnp.dot(po.astype(BF16), vnew_ref[0].astype(BF16), preferred_element_type=F32)
        acc_ref[...] = jnp.zeros(acc_ref.shape, F32)

    @pl.when(phase == 1)
    def _():
        acc = acc_ref[...]
        for k in range(pps):
            p = s_ref[c * pps + k].astype(BF16)
            acc = acc + jnp.dot(p, v_refs[k][0].astype(BF16), preferred_element_type=F32)
        acc_ref[...] = acc

    @pl.when((phase == 1) & (c == nc - 1))
    def _():
        full = (acc_ref[...] + own_ref[...]) / l_ref[...]
        for hh in range(MOBA_HEADS):
            o_ref[0, :, hh * MOBA_HD:(hh + 1) * MOBA_HD] = full[hh * t_new:(hh + 1) * t_new,
                                                                 hh * MOBA_HD:(hh + 1) * MOBA_HD]


def _moba_sample(qm, km_new, vm_new, cache_k, cache_v, page_table, pps):
    b, t_new, _ = qm.shape
    n_pages = page_table.shape[1]
    nc = n_pages // pps
    rows = MOBA_HEADS * t_new
    n_blk = n_pages * PAGE_SIZE // MOBA_BLOCK

    def k_spec(k):
        return pl.BlockSpec((1, PAGE_SIZE, MOBA_WIDTH),
                            lambda i, ph, c, pt: (pt[i, jnp.where(ph == 0, c, nc - 1) * pps + k], 0, 0))

    def v_spec(k):
        return pl.BlockSpec((1, PAGE_SIZE, MOBA_WIDTH),
                            lambda i, ph, c, pt: (pt[i, jnp.where(ph == 0, 0, c) * pps + k], 0, 0))

    tok = pl.BlockSpec((1, t_new, MOBA_WIDTH), lambda i, ph, c, pt: (i, 0, 0))
    grid_spec = pltpu.PrefetchScalarGridSpec(
        num_scalar_prefetch=1, grid=(b, 2, nc),
        in_specs=[tok, tok, tok] + [k_spec(k) for k in range(pps)] + [v_spec(k) for k in range(pps)],
        out_specs=tok,
        scratch_shapes=[pltpu.VMEM((rows, MOBA_WIDTH), BF16),
                        pltpu.VMEM((n_pages, rows, PAGE_SIZE), F32),
                        pltpu.VMEM((n_blk, MOBA_WIDTH), F32),
                        pltpu.VMEM((rows, 1), F32),
                        pltpu.VMEM((rows, MOBA_WIDTH), F32),
                        pltpu.VMEM((rows, MOBA_WIDTH), F32)])
    return pl.pallas_call(
        functools.partial(_moba_dec_kernel, pps, t_new, n_pages),
        grid_spec=grid_spec,
        out_shape=jax.ShapeDtypeStruct((b, t_new, MOBA_WIDTH), F32),
        compiler_params=_cparams(("arbitrary", "arbitrary", "arbitrary")),
        name="moba_dec",
    )(page_table, qm, km_new, vm_new, *([cache_k] * pps), *([cache_v] * pps))


def _oproj_kernel(x_ref, mod_ref, omla_ref, omoba_ref, bmla_ref, bmoba_ref, wo_ref, gffn_ref, wpq_ref,
                  x1_ref, h_ref, qp_ref):
    bb, ts, d = x_ref.shape
    m = bb * ts
    mod = mod_ref[...]
    a = (_rms(omla_ref[...]) * bmla_ref[...]).reshape(m, MLA_WIDTH).astype(BF16)
    b = (_rms(omoba_ref[...]) * bmoba_ref[...]).reshape(m, MOBA_WIDTH).astype(BF16)
    o = jnp.concatenate([a, b], axis=-1)
    y = jnp.dot(o, wo_ref[...], preferred_element_type=F32).reshape(bb, ts, d)
    x1 = x_ref[...] + mod[:, 2:3, :] * y
    x1_ref[...] = x1
    h = (_rms(x1) * gffn_ref[...] * (1.0 + mod[:, 4:5, :]) + mod[:, 3:4, :]).astype(BF16)
    h_ref[...] = h
    qp = jnp.dot(h.reshape(m, d), wpq_ref[...], preferred_element_type=F32)
    qp_ref[...] = qp.reshape(bb, ts, qp.shape[-1]).astype(BF16)


def _oproj(x, mod, o_mla, o_moba, w, bb, ts):
    b, s, d = x.shape
    pq = w["w_pq"].shape[1]

    def full(a):
        return pl.BlockSpec(a.shape, lambda i, j: (0,) * a.ndim)

    def tok(wd):
        return pl.BlockSpec((bb, ts, wd), lambda i, j: (i, j, 0))

    return pl.pallas_call(
        _oproj_kernel,
        grid=(b // bb, s // ts),
        in_specs=[tok(d), pl.BlockSpec((bb, 6, d), lambda i, j: (i, 0, 0)), tok(MLA_WIDTH), tok(MOBA_WIDTH),
                  full(w["beta_mla"]), full(w["beta_moba"]), full(w["w_o"]), full(w["g_ffn"]), full(w["w_pq"])],
        out_specs=[tok(d), tok(d), tok(pq)],
        out_shape=[jax.ShapeDtypeStruct((b, s, d), F32), jax.ShapeDtypeStruct((b, s, d), BF16),
                   jax.ShapeDtypeStruct((b, s, pq), BF16)],
        compiler_params=_cparams(("arbitrary", "arbitrary")),
        name="oproj",
    )(x, mod, o_mla, o_moba, w["beta_mla"], w["beta_moba"], w["w_o"], w["g_ffn"], w["w_pq"])


def _top_rows(s, k, payload=None):
    n = s.shape[0]
    row = lax.broadcasted_iota(I32, s.shape, 0)
    vals, idxs, picks = [], [], []
    for _ in range(k):
        mx = jnp.max(s, axis=0, keepdims=True)
        first = jnp.min(jnp.where(s == mx, row, n), axis=0, keepdims=True)
        hit = row == first
        vals.append(mx)
        idxs.append(first)
        if payload is not None:
            picks.append(jnp.max(jnp.where(hit, payload, -1), axis=0, keepdims=True))
        s = jnp.where(hit, -jnp.inf, s)
    cat = lambda xs: jnp.concatenate(xs, axis=0)
    return cat(vals), cat(idxs), (cat(picks) if payload is not None else None)


def _route_kernel(qp_ref, keys_ref, idx_ref, g_ref):
    nt = (((1,), (1,)), ((), ()))
    half = PEER_DKEY // 2
    for hh in range(PEER_HEADS):
        tops = []
        for p in range(2):
            lo = (hh * 2 + p) * half
            s = lax.dot_general(keys_ref[hh * 2 + p], qp_ref[:, lo:lo + half], nt,
                                preferred_element_type=F32)
            v, ix, _ = _top_rows(s, PEER_TOPK)
            tops.append((v, ix))
        (s1, i1), (s2, i2) = tops
        cand_s = jnp.concatenate([s1[a:a + 1] + s2 for a in range(PEER_TOPK)], axis=0)
        cand_i = jnp.concatenate([i1[a:a + 1] * PEER_NKEYS + i2 for a in range(PEER_TOPK)], axis=0)
        top_s, _, top_i = _top_rows(cand_s, PEER_TOPK, cand_i)
        e = jnp.exp(top_s - top_s[0:1])
        g_ref[hh * PEER_TOPK:(hh + 1) * PEER_TOPK, :] = e / jnp.sum(e, axis=0, keepdims=True)
        idx_ref[hh * PEER_TOPK:(hh + 1) * PEER_TOPK, :] = top_i


def _route(qp, keys, tm):
    n, pq = qp.shape
    return pl.pallas_call(
        _route_kernel,
        grid=(n // tm,),
        in_specs=[pl.BlockSpec((tm, pq), lambda i: (i, 0)),
                  pl.BlockSpec(keys.shape, lambda i: (0, 0, 0))],
        out_specs=[pl.BlockSpec((PEER_SEL, tm), lambda i: (0, i)), pl.BlockSpec((PEER_SEL, tm), lambda i: (0, i))],
        out_shape=[jax.ShapeDtypeStruct((PEER_SEL, n), I32), jax.ShapeDtypeStruct((PEER_SEL, n), F32)],
        compiler_params=_cparams(("arbitrary",)),
        name="route",
    )(qp, keys)


def _gelu_tanh(x):
    return 0.5 * x * (1.0 + jnp.tanh(0.7978845608028654 * (x + 0.044715 * (x * x * x))))


def _expert_kernel(tm, ec, h_ref, idx_ref, g_ref, u_ref, v_ref, o_ref, gate_ref):
    j = pl.program_id(1)
    nt = (((1,), (1,)), ((), ()))
    sub = ec // PEER_NKEYS

    @pl.when(j == 0)
    def _():
        rows = lax.broadcasted_iota(I32, (PEER_NKEYS, PEER_SEL), 0)

        def tok(n, carry):
            ix = idx_ref[pl.ds(n, 1), :]
            gg = g_ref[pl.ds(n, 1), :]
            w1 = jnp.where(rows == (ix >> 7), gg, 0.0).astype(BF16)
            w2 = jnp.where(rows == (ix & (PEER_NKEYS - 1)), 1.0, 0.0).astype(BF16)
            gate_ref[pl.ds(pl.multiple_of(n * GATE_PITCH, 8), PEER_NKEYS), :] = lax.dot_general(
                w1, w2, nt, preferred_element_type=F32)
            return carry

        lax.fori_loop(0, tm, tok, 0)
        o_ref[...] = jnp.zeros(o_ref.shape, F32)

    a = lax.dot_general(h_ref[...], u_ref[...], nt, preferred_element_type=F32)
    ws = []
    for e in range(sub):
        gate = gate_ref[pl.ds(j * sub + e, tm, stride=GATE_PITCH), :]
        ws.append((gate * _gelu_tanh(a[:, e * PEER_NKEYS:(e + 1) * PEER_NKEYS])).astype(BF16))
    w = jnp.concatenate(ws, axis=-1)
    o_ref[...] += jnp.dot(w, v_ref[...], preferred_element_type=F32)


def _expert(h, idx, g, u_tab, v_tab, tm, ec):
    n, d = h.shape
    ne = u_tab.shape[0]
    return pl.pallas_call(
        functools.partial(_expert_kernel, tm, ec),
        grid=(n // tm, ne // ec),
        in_specs=[pl.BlockSpec((tm, d), lambda i, j: (i, 0)),
                  pl.BlockSpec((tm, PEER_SEL), lambda i, j: (i, 0)),
                  pl.BlockSpec((tm, PEER_SEL), lambda i, j: (i, 0)),
                  pl.BlockSpec((ec, d), lambda i, j: (j, 0)),
                  pl.BlockSpec((ec, d), lambda i, j: (j, 0))],
        out_specs=pl.BlockSpec((tm, d), lambda i, j: (i, 0)),
        out_shape=jax.ShapeDtypeStruct((n, d), F32),
        scratch_shapes=[pltpu.VMEM((tm * GATE_PITCH, PEER_NKEYS), F32)],
        compiler_params=_cparams(("arbitrary", "arbitrary")),
        name="expert",
    )(h, idx, g, u_tab, v_tab)


def _final_kernel(last, x_ref, mod_ref, f_ref, gfin_ref, o_ref):
    x2 = x_ref[...] + mod_ref[...][:, 5:6, :] * f_ref[...]
    o_ref[...] = _rms(x2) * gfin_ref[...] if last else x2


def _final(x1, mod, f, g_final, bb, ts, last):
    b, s, d = x1.shape
    tok = pl.BlockSpec((bb, ts, d), lambda i, j: (i, j, 0))
    return pl.pallas_call(
        functools.partial(_final_kernel, last),
        grid=(b // bb, s // ts),
        in_specs=[tok, pl.BlockSpec((bb, 6, d), lambda i, j: (i, 0, 0)), tok,
                  pl.BlockSpec(g_final.shape, lambda i, j: (0, 0))],
        out_specs=tok,
        out_shape=jax.ShapeDtypeStruct((b, s, d), F32),
        compiler_params=_cparams(("arbitrary", "arbitrary")),
        name="final",
    )(x1, mod, f, g_final)


def _prep_weights(l, g_mix, w_in, g_q, w_uq, g_kv, w_uk, w_uv, beta_mla, beta_moba, w_o, g_ffn, w_pq,
                  sub_keys, u_tab, v_tab):
    d = w_in.shape[1]
    o_cq, o_ckv, o_kr = 0, MLA_Q_RANK, MLA_Q_RANK + MLA_KV_RANK
    o_qm = o_kr + MLA_ROPE
    o_km, o_vm = o_qm + MOBA_WIDTH, o_qm + 2 * MOBA_WIDTH
    wi = w_in[l]
    w_in_r = jnp.concatenate(
        [wi[:, o_cq:o_ckv], wi[:, o_ckv:o_kr], wi[:, o_qm:o_km], wi[:, o_km:o_vm], wi[:, o_vm:o_vm + MOBA_WIDTH],
         wi[:, o_kr:o_qm], jnp.zeros((d, LANES - MLA_ROPE), wi.dtype)], axis=1).astype(BF16)
    wq = w_uq[l].reshape(MLA_Q_RANK, MLA_HEADS, MLA_NOPE + MLA_ROPE)
    w_uq_r = jnp.concatenate([wq[:, :, :MLA_NOPE].reshape(MLA_Q_RANK, -1),
                              wq[:, :, MLA_NOPE:].reshape(MLA_Q_RANK, -1)], axis=1).astype(BF16)
    w_uk_t = jnp.transpose(w_uk[l], (1, 2, 0)).astype(BF16)
    wv = jnp.transpose(w_uv[l], (1, 0, 2))
    eye = jnp.eye(MLA_HEADS, dtype=wv.dtype)
    wuv_pad = (wv[:, :, None, :] * eye[:, None, :, None]).reshape(MLA_HEADS, MLA_KV_RANK, MLA_WIDTH).astype(BF16)
    row = lambda a: a[l].reshape(1, -1)
    return dict(
        g_mix=row(g_mix), w_in=w_in_r, g_q=row(g_q), w_uq=w_uq_r, g_kv=row(g_kv), w_uk_t=w_uk_t,
        wuv_pad=wuv_pad, beta_mla=row(beta_mla), beta_moba=row(beta_moba), w_o=w_o[l].astype(BF16),
        g_ffn=row(g_ffn), w_pq=w_pq[l].astype(BF16),
        keys=sub_keys[l].reshape(PEER_HEADS * 2, PEER_NKEYS, PEER_DKEY // 2).astype(BF16),
        u_tab=u_tab[l].astype(BF16), v_tab=v_tab[l].astype(BF16))


def _pick(n, pref):
    t = min(n, pref)
    while n % t:
        t -= 8
    return t


def kernel(x_prompt, x_sample, cache_mla_ckv, cache_mla_krope, cache_moba_k, cache_moba_v, page_table, c_prompt, c_sample, w_ada, b_ada, g_mix, w_in, g_q, w_uq, g_kv, w_uk, w_uv, beta_mla, beta_moba, w_o, g_ffn, w_pq, sub_keys, u_tab, v_tab, g_final):
    bp, sp, d = x_prompt.shape
    bs, ts, _ = x_sample.shape
    depth = w_ada.shape[0]
    n_pages = page_table.shape[1]
    past = n_pages * PAGE_SIZE
    assert sp % MOBA_BLOCK == 0 and past % MOBA_BLOCK == 0 and ts % 8 == 0
    pos_p = jnp.arange(sp, dtype=F32)
    pos_s = past + jnp.arange(ts, dtype=F32)
    n_pool = cache_moba_k.shape[1]
    gfin = g_final.reshape(1, d)

    xp, xs = x_prompt, x_sample
    new_p, new_s = [], []
    for l in range(depth):
        w = _prep_weights(l, g_mix, w_in, g_q, w_uq, g_kv, w_uk, w_uv, beta_mla, beta_moba, w_o, g_ffn, w_pq,
                          sub_keys, u_tab, v_tab)
        mod = _ada(jnp.concatenate([c_prompt, c_sample], axis=0), w_ada[l], b_ada[l])
        mod_p = mod[:bp].reshape(bp, 6, d)
        mod_s = mod[bp:].reshape(bs, 6, d)

        tp = _pick(sp, 512)
        qcat, kcat, ckv, krope, qm, kmb, vmb, km, vm, kmean = _proj(xp, mod_p, pos_p, w, 1, tp, True)
        o_mla = _mla_prompt(qcat, kcat, w["wuv_pad"], _pick(sp, 128), _pick(sp, 256))
        o_moba = _moba_prompt(qm, kmb, vmb, kmean.reshape(bp, sp // MOBA_BLOCK, MOBA_WIDTH))
        x1_p, h_p, qp_p = _oproj(xp, mod_p, o_mla, o_moba, w, 1, tp)
        new_p.append((ckv, krope, km.reshape(bp, sp, MOBA_HEADS, MOBA_HD), vm.reshape(bp, sp, MOBA_HEADS, MOBA_HD)))

        bbs = _pick(bs, 16)
        qcat_s, kcat_s, ckv_s, krope_s, qm_s, _, _, km_s, vm_s = _proj(xs, mod_s, pos_s, w, bbs, ts, False)
        o_mla_s = _mla_sample(qcat_s, kcat_s, w["wuv_pad"], cache_mla_ckv[l], cache_mla_krope[l], page_table,
                              _pick(n_pages, 16))
        o_moba_s = _moba_sample(qm_s, km_s, vm_s, cache_moba_k[l].reshape(n_pool, PAGE_SIZE, MOBA_WIDTH),
                                cache_moba_v[l].reshape(n_pool, PAGE_SIZE, MOBA_WIDTH), page_table,
                                _pick(n_pages, 8))
        x1_s, h_s, qp_s = _oproj(xs, mod_s, o_mla_s, o_moba_s, w, bbs, ts)
        new_s.append((ckv_s, krope_s, km_s.reshape(bs, ts, MOBA_HEADS, MOBA_HD),
                      vm_s.reshape(bs, ts, MOBA_HEADS, MOBA_HD)))

        n_p, n_s = bp * sp, bs * ts
        qp = jnp.concatenate([qp_p.reshape(n_p, -1), qp_s.reshape(n_s, -1)], axis=0)
        tr = _pick(n_p + n_s, 256)
        idx_t, g_t = _route(qp, w["keys"], tr)
        h_all = jnp.concatenate([h_p.reshape(n_p, d), h_s.reshape(n_s, d)], axis=0)
        f = _expert(h_all, idx_t.T, g_t.T, w["u_tab"], w["v_tab"], tr, 512)
        last = l == depth - 1
        xp = _final(x1_p, mod_p, f[:n_p].reshape(bp, sp, d), gfin, 1, tp, last)
        xs = _final(x1_s, mod_s, f[n_p:].reshape(bs, ts, d), gfin, bbs, ts, last)

    def stack(sts, i):
        return jnp.stack([st[i] for st in sts])

    return (xp, xs, stack(new_p, 0), stack(new_p, 1), stack(new_p, 2), stack(new_p, 3),
            stack(new_s, 0), stack(new_s, 1), stack(new_s, 2), stack(new_s, 3))
```

```python
import functools

import jax
import jax.numpy as jnp
from jax import lax
from jax.experimental import pallas as pl
from jax.experimental.pallas import tpu as pltpu

F32 = jnp.float32
BF16 = jnp.bfloat16
I32 = jnp.int32

EPS = 1e-6
NEG = -1e30
ROPE_THETA = 500000.0
PAGE_SIZE = 128
MLA_HEADS = 8
MLA_NOPE = 64
MLA_ROPE = 32
MLA_V = 64
MLA_Q_RANK = 256
MLA_KV_RANK = 128
MLA_QK = MLA_KV_RANK + MLA_ROPE
MOBA_HEADS = 8
MOBA_HD = 64
MOBA_ROT = MOBA_HD // 4
MOBA_BLOCK = 256
MOBA_TOPK = 3
MOBA_WIDTH = MOBA_HEADS * MOBA_HD
MLA_WIDTH = MLA_HEADS * MLA_V
PEER_HEADS = 8
PEER_NKEYS = 128
PEER_DKEY = 256
PEER_TOPK = 16
PEER_SEL = PEER_HEADS * PEER_TOPK

LANES = 128
V7X_VMEM_BYTES = 64 * 1024 * 1024
VMEM_LIMIT = V7X_VMEM_BYTES * 7 // 8
GATE_PITCH = PEER_NKEYS + 8

_C_CQ = 0
_C_CKV = _C_CQ + MLA_Q_RANK
_C_QM = _C_CKV + MLA_KV_RANK
_C_KM = _C_QM + MOBA_WIDTH
_C_VM = _C_KM + MOBA_WIDTH
_C_KR = _C_VM + MOBA_WIDTH
_C_END = _C_KR + LANES


def _cparams(sem):
    return pltpu.CompilerParams(dimension_semantics=sem, vmem_limit_bytes=VMEM_LIMIT)


def _rms(x):
    return x * lax.rsqrt(jnp.mean(x * x, axis=-1, keepdims=True) + EPS)


def _rope(x, tab_ref, half):
    n = x.shape[-1]
    c, s1, s2 = tab_ref[0], tab_ref[1], tab_ref[2]
    return x * c + pltpu.roll(x, n - half, 1) * s1 + pltpu.roll(x, half, 1) * s2


def _ada_kernel(c_ref, w_ref, b_ref, o_ref):
    c = c_ref[...]
    a = (c * jax.nn.sigmoid(c)).astype(BF16)
    o_ref[...] = jnp.dot(a, w_ref[...].astype(BF16), preferred_element_type=F32) + b_ref[...]


def _ada(c, w, b):
    n, d = c.shape
    width = w.shape[1]
    tn = width // 4
    return pl.pallas_call(
        _ada_kernel,
        grid=(width // tn,),
        in_specs=[pl.BlockSpec((n, d), lambda j: (0, 0)),
                  pl.BlockSpec((d, tn), lambda j: (0, j)),
                  pl.BlockSpec((1, tn), lambda j: (0, j))],
        out_specs=pl.BlockSpec((n, tn), lambda j: (0, j)),
        out_shape=jax.ShapeDtypeStruct((n, width), F32),
        compiler_params=_cparams(("arbitrary",)),
        name="ada",
    )(c, w, b.reshape(1, width))


def _proj_kernel(n_mean, x_ref, mod_ref, gmix_ref, win_ref, gq_ref, wuq_ref, gkv_ref, wuk_ref,
                 tmla_ref, tkr_ref, tmob_ref,
                 qcat_ref, kcat_ref, ckv_ref, krope_ref, qm_ref, kmb_ref, vmb_ref, km_ref, vm_ref,
                 *maybe_kmean_ref):
    bb, ts, d = x_ref.shape
    m = bb * ts
    mod = mod_ref[...]
    h = _rms(x_ref[...]) * gmix_ref[...] * (1.0 + mod[:, 1:2, :]) + mod[:, 0:1, :]
    z = jnp.dot(h.reshape(m, d).astype(BF16), win_ref[...], preferred_element_type=F32)

    def rope(x, ref, half):
        if bb == 1:
            return _rope(x, ref, half)
        w = x.shape[-1]
        x3 = x.reshape(bb, ts, w)
        r1 = pltpu.roll(x, w - half, 1).reshape(bb, ts, w)
        r2 = pltpu.roll(x, half, 1).reshape(bb, ts, w)
        return (x3 * ref[0][None] + r1 * ref[1][None] + r2 * ref[2][None]).reshape(m, w)

    cq = _rms(z[:, _C_CQ:_C_CQ + MLA_Q_RANK]) * gq_ref[...]
    q = jnp.dot(cq.astype(BF16), wuq_ref[...], preferred_element_type=F32)
    scale = (MLA_NOPE + MLA_ROPE) ** -0.5
    nope_w = MLA_HEADS * MLA_NOPE
    q_rope = rope(q[:, nope_w:], tmla_ref, MLA_ROPE // 2) * scale
    for hh in range(MLA_HEADS):
        qn = q[:, hh * MLA_NOPE:(hh + 1) * MLA_NOPE].astype(BF16)
        ql = jnp.dot(qn, wuk_ref[hh], preferred_element_type=F32) * scale
        qcat_ref[:, hh, :, 0:MLA_KV_RANK] = ql.reshape(bb, ts, MLA_KV_RANK).astype(BF16)
        qr = q_rope[:, hh * MLA_ROPE:(hh + 1) * MLA_ROPE]
        qcat_ref[:, hh, :, MLA_KV_RANK:MLA_QK] = qr.reshape(bb, ts, MLA_ROPE).astype(BF16)

    ckv = _rms(z[:, _C_CKV:_C_CKV + MLA_KV_RANK]) * gkv_ref[...]
    kr = rope(z[:, _C_KR:_C_KR + LANES], tkr_ref, MLA_ROPE // 2)[:, 0:MLA_ROPE]
    ckv_ref[...] = ckv.reshape(bb, ts, MLA_KV_RANK)
    krope_ref[...] = kr.reshape(bb, ts, MLA_ROPE)
    kcat_ref[:, :, 0:MLA_KV_RANK] = ckv.reshape(bb, ts, MLA_KV_RANK).astype(BF16)
    kcat_ref[:, :, MLA_KV_RANK:MLA_QK] = kr.reshape(bb, ts, MLA_ROPE).astype(BF16)

    qm = rope(z[:, _C_QM:_C_QM + MOBA_WIDTH], tmob_ref, MOBA_ROT // 2) * (MOBA_HD ** -0.5)
    km = rope(z[:, _C_KM:_C_KM + MOBA_WIDTH], tmob_ref, MOBA_ROT // 2)
    vm = z[:, _C_VM:_C_VM + MOBA_WIDTH]
    qm_ref[...] = qm.reshape(bb, ts, MOBA_WIDTH).astype(BF16)
    kmb_ref[...] = km.reshape(bb, ts, MOBA_WIDTH).astype(BF16)
    vmb_ref[...] = vm.reshape(bb, ts, MOBA_WIDTH).astype(BF16)
    km_ref[...] = km.reshape(bb, ts, MOBA_WIDTH)
    vm_ref[...] = vm.reshape(bb, ts, MOBA_WIDTH)
    if n_mean:
        kmean_ref, = maybe_kmean_ref
        for i in range(n_mean):
            blk = km[i * MOBA_BLOCK:(i + 1) * MOBA_BLOCK]
            kmean_ref[0, 0, i:i + 1, :] = jnp.sum(blk, axis=0, keepdims=True) * (1.0 / MOBA_BLOCK)


def _proj(x, mod, pos, w, bb, ts, with_kmean):
    b, s, d = x.shape
    nb, nt = b // bb, s // ts
    n_mean = ts // MOBA_BLOCK if with_kmean else 0
    tmla = _rope_tables(pos, MLA_ROPE, MLA_ROPE, MLA_HEADS)
    tkr = _rope_tables(pos, MLA_ROPE, LANES, 1)
    tmob = _rope_tables(pos, MOBA_ROT, MOBA_HD, MOBA_HEADS)

    def full(a):
        return pl.BlockSpec(a.shape, lambda i, j: (0,) * a.ndim)

    def tok(wd):
        return pl.BlockSpec((bb, ts, wd), lambda i, j: (i, j, 0))

    def tab(wd):
        return pl.BlockSpec((3, ts, wd), lambda i, j: (0, j, 0))

    out_shape = [jax.ShapeDtypeStruct((b, MLA_HEADS, s, MLA_QK), BF16),
                 jax.ShapeDtypeStruct((b, s, MLA_QK), BF16),
                 jax.ShapeDtypeStruct((b, s, MLA_KV_RANK), F32),
                 jax.ShapeDtypeStruct((b, s, MLA_ROPE), F32),
                 jax.ShapeDtypeStruct((b, s, MOBA_WIDTH), BF16),
                 jax.ShapeDtypeStruct((b, s, MOBA_WIDTH), BF16),
                 jax.ShapeDtypeStruct((b, s, MOBA_WIDTH), BF16),
                 jax.ShapeDtypeStruct((b, s, MOBA_WIDTH), F32),
                 jax.ShapeDtypeStruct((b, s, MOBA_WIDTH), F32)]
    out_specs = [pl.BlockSpec((bb, MLA_HEADS, ts, MLA_QK), lambda i, j: (i, 0, j, 0)),
                 tok(MLA_QK), tok(MLA_KV_RANK), tok(MLA_ROPE),
                 tok(MOBA_WIDTH), tok(MOBA_WIDTH), tok(MOBA_WIDTH), tok(MOBA_WIDTH), tok(MOBA_WIDTH)]
    if n_mean:
        out_shape.append(jax.ShapeDtypeStruct((b, nt, n_mean, MOBA_WIDTH), F32))
        out_specs.append(pl.BlockSpec((1, 1, n_mean, MOBA_WIDTH), lambda i, j: (i, j, 0, 0)))
    args = (x, mod, w["g_mix"], w["w_in"], w["g_q"], w["w_uq"], w["g_kv"], w["w_uk_t"], tmla, tkr, tmob)
    in_specs = [tok(d), pl.BlockSpec((bb, 6, d), lambda i, j: (i, 0, 0)),
                full(w["g_mix"]), full(w["w_in"]), full(w["g_q"]), full(w["w_uq"]), full(w["g_kv"]),
                full(w["w_uk_t"]), tab(MLA_HEADS * MLA_ROPE), tab(LANES), tab(MOBA_WIDTH)]
    return pl.pallas_call(
        functools.partial(_proj_kernel, n_mean),
        grid=(nb, nt), in_specs=in_specs, out_specs=out_specs, out_shape=out_shape,
        compiler_params=_cparams(("arbitrary", "arbitrary")),
        name="proj",
    )(*args)


def _rope_tables(pos, rot, unit, rep):
    half = rot // 2
    inv = ROPE_THETA ** (-jnp.arange(half, dtype=F32) / half)
    ang = pos[:, None] * inv[None, :]
    cos, sin = jnp.cos(ang), jnp.sin(ang)
    n = pos.shape[0]
    one, zero, zh = jnp.ones((n, unit - rot), F32), jnp.zeros((n, unit - rot), F32), jnp.zeros((n, half), F32)
    c = jnp.concatenate([cos, cos, one], axis=1)
    s1 = jnp.concatenate([-sin, zh, zero], axis=1)
    s2 = jnp.concatenate([zh, sin, zero], axis=1)
    return jnp.stack([jnp.tile(t, (1, rep)) for t in (c, s1, s2)])


def _mla_kernel(tq, tk, q_ref, k_ref, wuv_ref, o_ref):
    i = pl.program_id(1)
    rows = MLA_HEADS * tq
    q = q_ref[0].reshape(rows, MLA_QK)
    q0 = i * tq

    def step(j, carry, masked):
        m, l, acc = carry
        kb = k_ref[0, pl.ds(pl.multiple_of(j * tk, tk), tk), :]
        s = lax.dot_general(q, kb, (((1,), (1,)), ((), ())), preferred_element_type=F32)
        if masked:
            qpos = q0 + lax.broadcasted_iota(I32, (MLA_HEADS, tq, tk), 1).reshape(rows, tk)
            kpos = j * tk + lax.broadcasted_iota(I32, (rows, tk), 1)
            s = jnp.where(kpos <= qpos, s, NEG)
        m_new = jnp.maximum(m, jnp.max(s, axis=-1, keepdims=True))
        alpha = jnp.exp(m - m_new)
        p = jnp.exp(s - m_new)
        l = alpha * l + jnp.sum(p, axis=-1, keepdims=True)
        acc = alpha * acc + jnp.dot(p.astype(BF16), kb[:, 0:MLA_KV_RANK], preferred_element_type=F32)
        return m_new, l, acc

    init = (jnp.full((rows, 1), NEG, F32), jnp.zeros((rows, 1), F32), jnp.zeros((rows, MLA_KV_RANK), F32))
    nfull = q0 // tk
    carry = lax.fori_loop(0, nfull, lambda j, c: step(j, c, False), init)
    m, l, acc = step(nfull, carry, True)
    o_lat = (acc / l).astype(BF16)
    out = jnp.zeros((tq, MLA_WIDTH), F32)
    for hh in range(MLA_HEADS):
        out = out + jnp.dot(o_lat[hh * tq:(hh + 1) * tq], wuv_ref[hh], preferred_element_type=F32)
    o_ref[0] = out


def _mla_prompt(qcat, kcat, wuv_pad, tq, tk):
    b, _, s, _ = qcat.shape
    return pl.pallas_call(
        functools.partial(_mla_kernel, tq, tk),
        grid=(b, s // tq),
        in_specs=[pl.BlockSpec((1, MLA_HEADS, tq, MLA_QK), lambda i, j: (i, 0, j, 0)),
                  pl.BlockSpec((1, s, MLA_QK), lambda i, j: (i, 0, 0)),
                  pl.BlockSpec(wuv_pad.shape, lambda i, j: (0, 0, 0))],
        out_specs=pl.BlockSpec((1, tq, MLA_WIDTH), lambda i, j: (i, j, 0)),
        out_shape=jax.ShapeDtypeStruct((b, s, MLA_WIDTH), F32),
        compiler_params=_cparams(("arbitrary", "arbitrary")),
        name="mla",
    )(qcat, kcat, wuv_pad)


def _moba_kernel(nb, q_ref, k_ref, v_ref, kmean_ref, o_ref):
    i = pl.program_id(1)
    tq = MOBA_BLOCK
    col = lax.broadcasted_iota(I32, (tq, nb), 1)
    qpos = lax.broadcasted_iota(I32, (tq, tq), 0)
    kpos = lax.broadcasted_iota(I32, (tq, tq), 1)
    for hh in range(MOBA_HEADS):
        lo = hh * MOBA_HD
        q = q_ref[0, :, lo:lo + MOBA_HD]
        kmean = kmean_ref[0, :, lo:lo + MOBA_HD].astype(BF16)
        gs = lax.dot_general(q, kmean, (((1,), (1,)), ((), ())), preferred_element_type=F32)
        gs = jnp.where(col < i, gs, NEG)
        cnt = jnp.zeros((tq, nb), F32)
        for jj in range(nb):
            cj = gs[:, jj:jj + 1]
            cnt = cnt + jnp.where(cj > gs, 1.0, 0.0) + jnp.where((cj == gs) & (jj < col), 1.0, 0.0)
        sel = jnp.where((cnt < MOBA_TOPK) & (col < i), 1.0, 0.0)

        def step(j, carry):
            m, l, acc = carry
            off = pl.multiple_of(j * tq, tq)
            kb = k_ref[0, pl.ds(off, tq), lo:lo + MOBA_HD]
            vb = v_ref[0, pl.ds(off, tq), lo:lo + MOBA_HD]
            s = lax.dot_general(q, kb, (((1,), (1,)), ((), ())), preferred_element_type=F32)
            selcol = jnp.sum(jnp.where(col == j, sel, 0.0), axis=-1, keepdims=True)
            selcol = jnp.where(j == i, 1.0, selcol)
            s = jnp.where(kpos + (j - i) * tq <= qpos, s, NEG)
            s = jnp.where(selcol > 0.5, s, NEG)
            m_new = jnp.maximum(m, jnp.max(s, axis=-1, keepdims=True))
            alpha = jnp.exp(m - m_new)
            p = jnp.where(s > 0.5 * NEG, jnp.exp(s - m_new), 0.0)
            l = alpha * l + jnp.sum(p, axis=-1, keepdims=True)
            acc = alpha * acc + jnp.dot(p.astype(BF16), vb, preferred_element_type=F32)
            return m_new, l, acc

        init = (jnp.full((tq, 1), NEG, F32), jnp.zeros((tq, 1), F32), jnp.zeros((tq, MOBA_HD), F32))
        m, l, acc = lax.fori_loop(0, i + 1, step, init)
        o_ref[0, :, lo:lo + MOBA_HD] = acc / l


def _moba_prompt(qm, kmb, vmb, kmean):
    b, s, _ = qm.shape
    nb = s // MOBA_BLOCK
    return pl.pallas_call(
        functools.partial(_moba_kernel, nb),
        grid=(b, nb),
        in_specs=[pl.BlockSpec((1, MOBA_BLOCK, MOBA_WIDTH), lambda i, j: (i, j, 0)),
                  pl.BlockSpec((1, s, MOBA_WIDTH), lambda i, j: (i, 0, 0)),
                  pl.BlockSpec((1, s, MOBA_WIDTH), lambda i, j: (i, 0, 0)),
                  pl.BlockSpec((1, nb, MOBA_WIDTH), lambda i, j: (i, 0, 0))],
        out_specs=pl.BlockSpec((1, MOBA_BLOCK, MOBA_WIDTH), lambda i, j: (i, j, 0)),
        out_shape=jax.ShapeDtypeStruct((b, s, MOBA_WIDTH), F32),
        compiler_params=_cparams(("arbitrary", "arbitrary")),
        name="moba",
    )(qm, kmb, vmb, kmean)


def _mla_dec_kernel(pps, t_new, pt_ref, q_ref, knew_ref, wuv_ref, *rest):
    ckv_refs, kr_refs = rest[:pps], rest[pps:2 * pps]
    o_ref, m_ref, l_ref, acc_ref = rest[2 * pps:]
    c = pl.program_id(1)
    rows = MLA_HEADS * t_new
    q = q_ref[0].reshape(rows, MLA_QK)
    ql, qr = q[:, 0:MLA_KV_RANK], q[:, MLA_KV_RANK:MLA_QK]

    @pl.when(c == 0)
    def _():
        m_ref[...] = jnp.full(m_ref.shape, NEG, F32)
        l_ref[...] = jnp.zeros(l_ref.shape, F32)
        acc_ref[...] = jnp.zeros(acc_ref.shape, F32)

    nt = (((1,), (1,)), ((), ()))
    kc = jnp.concatenate([r[0].astype(BF16) for r in ckv_refs], axis=0)
    krt = jnp.concatenate([r[0].astype(BF16) for r in kr_refs], axis=1)
    s = (lax.dot_general(ql, kc, nt, preferred_element_type=F32)
         + jnp.dot(qr, krt, preferred_element_type=F32))
    m_old = m_ref[...]
    m_new = jnp.maximum(m_old, jnp.max(s, axis=-1, keepdims=True))
    alpha = jnp.exp(m_old - m_new)
    p = jnp.exp(s - m_new)
    l_ref[...] = alpha * l_ref[...] + jnp.sum(p, axis=-1, keepdims=True)
    acc_ref[...] = alpha * acc_ref[...] + jnp.dot(p.astype(BF16), kc, preferred_element_type=F32)
    m_ref[...] = m_new

    @pl.when(c == pl.num_programs(1) - 1)
    def _():
        kn = knew_ref[0]
        sn = lax.dot_general(q, kn, nt, preferred_element_type=F32)
        tq = lax.broadcasted_iota(I32, (MLA_HEADS, t_new, t_new), 1).reshape(rows, t_new)
        tk = lax.broadcasted_iota(I32, (rows, t_new), 1)
        sn = jnp.where(tk <= tq, sn, NEG)
        m1 = m_ref[...]
        m2 = jnp.maximum(m1, jnp.max(sn, axis=-1, keepdims=True))
        a2 = jnp.exp(m1 - m2)
        pn = jnp.exp(sn - m2)
        l2 = a2 * l_ref[...] + jnp.sum(pn, axis=-1, keepdims=True)
        acc2 = a2 * acc_ref[...] + jnp.dot(pn.astype(BF16), kn[:, 0:MLA_KV_RANK], preferred_element_type=F32)
        o_lat = (acc2 / l2).astype(BF16)
        out = jnp.zeros((t_new, MLA_WIDTH), F32)
        for hh in range(MLA_HEADS):
            out = out + jnp.dot(o_lat[hh * t_new:(hh + 1) * t_new], wuv_ref[hh], preferred_element_type=F32)
        o_ref[0] = out


def _mla_sample(qcat, kcat_new, wuv_pad, cache_ckv, cache_kr, page_table, pps):
    b, _, t_new, _ = qcat.shape
    n_pages = page_table.shape[1]
    rows = MLA_HEADS * t_new

    def page_spec(k, shape):
        return pl.BlockSpec((1,) + shape, lambda i, c, pt: (pt[i, c * pps + k], 0, 0))

    in_specs = ([pl.BlockSpec((1, MLA_HEADS, t_new, MLA_QK), lambda i, c, pt: (i, 0, 0, 0)),
                 pl.BlockSpec((1, t_new, MLA_QK), lambda i, c, pt: (i, 0, 0)),
                 pl.BlockSpec(wuv_pad.shape, lambda i, c, pt: (0, 0, 0))]
                + [page_spec(k, (PAGE_SIZE, MLA_KV_RANK)) for k in range(pps)]
                + [page_spec(k, (MLA_ROPE, PAGE_SIZE)) for k in range(pps)])
    grid_spec = pltpu.PrefetchScalarGridSpec(
        num_scalar_prefetch=1, grid=(b, n_pages // pps), in_specs=in_specs,
        out_specs=pl.BlockSpec((1, t_new, MLA_WIDTH), lambda i, c, pt: (i, 0, 0)),
        scratch_shapes=[pltpu.VMEM((rows, 1), F32), pltpu.VMEM((rows, 1), F32),
                        pltpu.VMEM((rows, MLA_KV_RANK), F32)])
    return pl.pallas_call(
        functools.partial(_mla_dec_kernel, pps, t_new),
        grid_spec=grid_spec,
        out_shape=jax.ShapeDtypeStruct((b, t_new, MLA_WIDTH), F32),
        compiler_params=_cparams(("arbitrary", "arbitrary")),
        name="mla_dec",
    )(page_table, qcat, kcat_new, wuv_pad, *([cache_ckv] * pps), *([cache_kr] * pps))


def _moba_dec_kernel(pps, t_new, n_pages, pt_ref, q_ref, knew_ref, vnew_ref, *rest):
    k_refs, v_refs = rest[:pps], rest[pps:2 * pps]
    o_ref, qbd_ref, s_ref, p_ref, gs_ref, l_ref, own_ref, acc_ref = rest[2 * pps:]
    phase, c = pl.program_id(1), pl.program_id(2)
    nc = n_pages // pps
    rows = MOBA_HEADS * t_new
    n_blk = n_pages * PAGE_SIZE // MOBA_BLOCK
    bps = pps * PAGE_SIZE // MOBA_BLOCK
    nt = (((1,), (1,)), ((), ()))
    bcol = lax.broadcasted_iota(I32, (rows, n_blk), 1)

    @pl.when((phase == 0) & (c == 0))
    def _():
        qt = jnp.concatenate([q_ref[0].astype(F32)] * MOBA_HEADS, axis=0)
        rh = lax.broadcasted_iota(I32, (MOBA_HEADS, t_new, MOBA_WIDTH), 0).reshape(rows, MOBA_WIDTH)
        ch = lax.broadcasted_iota(I32, (rows, MOBA_HEADS, MOBA_HD), 1).reshape(rows, MOBA_WIDTH)
        qbd_ref[...] = jnp.where(rh == ch, qt, 0.0).astype(BF16)
        gs_ref[...] = jnp.zeros(gs_ref.shape, F32)

    @pl.when(phase == 0)
    def _():
        kt = jnp.concatenate([r[0].astype(BF16) for r in k_refs], axis=1)
        s = jnp.dot(qbd_ref[...], kt, preferred_element_type=F32)
        s_ref[c] = s
        gs = gs_ref[...]
        for kb in range(bps):
            gsum = jnp.sum(s[:, kb * MOBA_BLOCK:(kb + 1) * MOBA_BLOCK], axis=-1, keepdims=True)
            gs = jnp.where(bcol == c * bps + kb, gsum, gs)
        gs_ref[...] = gs

    @pl.when((phase == 0) & (c == nc - 1))
    def _():
        qbd = qbd_ref[...]
        gs = gs_ref[...]
        sel = jnp.zeros((rows, n_blk), F32)
        for _ in range(min(MOBA_TOPK, n_blk)):
            mx = jnp.max(gs, axis=-1, keepdims=True)
            first = jnp.min(jnp.where(gs == mx, bcol, n_blk), axis=-1, keepdims=True)
            hit = bcol == first
            sel = jnp.where(hit, 1.0, sel)
            gs = jnp.where(hit, -jnp.inf, gs)
        kn = knew_ref[0].astype(BF16)
        so = lax.dot_general(qbd, kn, nt, preferred_element_type=F32)
        tq = lax.broadcasted_iota(I32, (MOBA_HEADS, t_new, t_new), 1).reshape(rows, t_new)
        tk = lax.broadcasted_iota(I32, (rows, t_new), 1)
        so = jnp.where(tk <= tq, so, NEG)

        def block_scores(blk):
            lo = (blk % bps) * MOBA_BLOCK
            return s_ref[blk // bps, :, lo:lo + MOBA_BLOCK], sel[:, blk:blk + 1] > 0.5

        half = MOBA_BLOCK // 2
        m_el = jnp.full((rows, half), NEG, F32)
        for blk in range(n_blk):
            sb, keep = block_scores(blk)
            m_el = jnp.maximum(m_el, jnp.where(keep, jnp.maximum(sb[:, :half], sb[:, half:]), NEG))
        m = jnp.maximum(jnp.max(m_el, axis=-1, keepdims=True), jnp.max(so, axis=-1, keepdims=True))
        l_el = jnp.zeros((rows, half), F32)
        for blk in range(n_blk):
            sb, keep = block_scores(blk)
            p = jnp.where(keep, jnp.exp(jnp.where(keep, sb, NEG) - m), 0.0)
            l_el = l_el + p[:, :half] + p[:, half:]
            lo = (blk % bps) * MOBA_BLOCK
            p_ref[blk // bps, :, lo:lo + MOBA_BLOCK] = p.astype(BF16)
        po = jnp.exp(so - m)
        l_ref[...] = jnp.sum(l_el, axis=-1, keepdims=True) + jnp.sum(po, axis=-1, keepdims=True)
        own_ref[...] = jnp.dot(po.astype(BF16), vnew_ref[0].astype(BF16), preferred_element_type=F32)
        acc_ref[...] = jnp.zeros(acc_ref.shape, F32)

    @pl.when(phase == 1)
    def _():
        vt = jnp.concatenate([r[0].astype(BF16) for r in v_refs], axis=1)
        acc_ref[...] += lax.dot_general(p_ref[c], vt, nt, preferred_element_type=F32)

    @pl.when((phase == 1) & (c == nc - 1))
    def _():
        full = (acc_ref[...] + own_ref[...]) / l_ref[...]
        for hh in range(MOBA_HEADS):
            o_ref[0, :, hh * MOBA_HD:(hh + 1) * MOBA_HD] = full[hh * t_new:(hh + 1) * t_new,
                                                                 hh * MOBA_HD:(hh + 1) * MOBA_HD]


def _moba_sample(qm, km_new, vm_new, cache_k, cache_v, page_table, pps):
    b, t_new, _ = qm.shape
    n_pages = page_table.shape[1]
    nc = n_pages // pps
    rows = MOBA_HEADS * t_new
    n_blk = n_pages * PAGE_SIZE // MOBA_BLOCK

    def k_spec(k):
        return pl.BlockSpec((1, MOBA_WIDTH, PAGE_SIZE),
                            lambda i, ph, c, pt: (pt[i, jnp.where(ph == 0, c, nc - 1) * pps + k], 0, 0))

    def v_spec(k):
        return pl.BlockSpec((1, MOBA_WIDTH, PAGE_SIZE),
                            lambda i, ph, c, pt: (pt[i, jnp.where(ph == 0, 0, c) * pps + k], 0, 0))

    tok = pl.BlockSpec((1, t_new, MOBA_WIDTH), lambda i, ph, c, pt: (i, 0, 0))
    grid_spec = pltpu.PrefetchScalarGridSpec(
        num_scalar_prefetch=1, grid=(b, 2, nc),
        in_specs=[tok, tok, tok] + [k_spec(k) for k in range(pps)] + [v_spec(k) for k in range(pps)],
        out_specs=tok,
        scratch_shapes=[pltpu.VMEM((rows, MOBA_WIDTH), BF16),
                        pltpu.VMEM((nc, rows, pps * PAGE_SIZE), F32),
                        pltpu.VMEM((nc, rows, pps * PAGE_SIZE), BF16),
                        pltpu.VMEM((rows, n_blk), F32),
                        pltpu.VMEM((rows, 1), F32),
                        pltpu.VMEM((rows, MOBA_WIDTH), F32),
                        pltpu.VMEM((rows, MOBA_WIDTH), F32)])
    return pl.pallas_call(
        functools.partial(_moba_dec_kernel, pps, t_new, n_pages),
        grid_spec=grid_spec,
        out_shape=jax.ShapeDtypeStruct((b, t_new, MOBA_WIDTH), F32),
        compiler_params=_cparams(("arbitrary", "arbitrary", "arbitrary")),
        name="moba_dec",
    )(page_table, qm, km_new, vm_new, *([cache_k] * pps), *([cache_v] * pps))


def _oproj_kernel(x_ref, mod_ref, omla_ref, omoba_ref, bmla_ref, bmoba_ref, wo_ref, gffn_ref, wpq_ref,
                  x1_ref, h_ref, qp_ref):
    bb, ts, d = x_ref.shape
    m = bb * ts
    mod = mod_ref[...]
    a = (_rms(omla_ref[...]) * bmla_ref[...]).reshape(m, MLA_WIDTH).astype(BF16)
    b = (_rms(omoba_ref[...]) * bmoba_ref[...]).reshape(m, MOBA_WIDTH).astype(BF16)
    o = jnp.concatenate([a, b], axis=-1)
    y = jnp.dot(o, wo_ref[...], preferred_element_type=F32).reshape(bb, ts, d)
    x1 = x_ref[...] + mod[:, 2:3, :] * y
    x1_ref[...] = x1
    h = (_rms(x1) * gffn_ref[...] * (1.0 + mod[:, 4:5, :]) + mod[:, 3:4, :]).astype(BF16)
    h_ref[...] = h
    qp = jnp.dot(h.reshape(m, d), wpq_ref[...], preferred_element_type=F32)
    qp_ref[...] = qp.reshape(bb, ts, qp.shape[-1]).astype(BF16)


def _oproj(x, mod, o_mla, o_moba, w, bb, ts):
    b, s, d = x.shape
    pq = w["w_pq"].shape[1]

    def full(a):
        return pl.BlockSpec(a.shape, lambda i, j: (0,) * a.ndim)

    def tok(wd):
        return pl.BlockSpec((bb, ts, wd), lambda i, j: (i, j, 0))

    return pl.pallas_call(
        _oproj_kernel,
        grid=(b // bb, s // ts),
        in_specs=[tok(d), pl.BlockSpec((bb, 6, d), lambda i, j: (i, 0, 0)), tok(MLA_WIDTH), tok(MOBA_WIDTH),
                  full(w["beta_mla"]), full(w["beta_moba"]), full(w["w_o"]), full(w["g_ffn"]), full(w["w_pq"])],
        out_specs=[tok(d), tok(d), tok(pq)],
        out_shape=[jax.ShapeDtypeStruct((b, s, d), F32), jax.ShapeDtypeStruct((b, s, d), BF16),
                   jax.ShapeDtypeStruct((b, s, pq), BF16)],
        compiler_params=_cparams(("arbitrary", "arbitrary")),
        name="oproj",
    )(x, mod, o_mla, o_moba, w["beta_mla"], w["beta_moba"], w["w_o"], w["g_ffn"], w["w_pq"])


def _top_rows(s, k, payload=None):
    n = s.shape[0]
    row = lax.broadcasted_iota(I32, s.shape, 0)
    vals, idxs, picks = [], [], []
    for _ in range(k):
        mx = jnp.max(s, axis=0, keepdims=True)
        first = jnp.min(jnp.where(s == mx, row, n), axis=0, keepdims=True)
        hit = row == first
        vals.append(mx)
        idxs.append(first)
        if payload is not None:
            picks.append(jnp.max(jnp.where(hit, payload, -1), axis=0, keepdims=True))
        s = jnp.where(hit, -jnp.inf, s)
    cat = lambda xs: jnp.concatenate(xs, axis=0)
    return cat(vals), cat(idxs), (cat(picks) if payload is not None else None)


def _route_kernel(qp_ref, keys_ref, idx_ref, g_ref):
    nt = (((1,), (1,)), ((), ()))
    half = PEER_DKEY // 2
    for hh in range(PEER_HEADS):
        tops = []
        for p in range(2):
            lo = (hh * 2 + p) * half
            s = lax.dot_general(keys_ref[hh * 2 + p], qp_ref[:, lo:lo + half], nt,
                                preferred_element_type=F32)
            v, ix, _ = _top_rows(s, PEER_TOPK)
            tops.append((v, ix))
        (s1, i1), (s2, i2) = tops
        slabs_s, slabs_i = [], []
        for a in range(PEER_TOPK):
            nb = PEER_TOPK // (a + 1)
            keep = PEER_TOPK if nb > 8 else 8
            ss = s1[a:a + 1] + s2[0:keep]
            if nb < keep:
                ss = jnp.where(lax.broadcasted_iota(I32, ss.shape, 0) < nb, ss, -jnp.inf)
            slabs_s.append(ss)
            slabs_i.append(i1[a:a + 1] * PEER_NKEYS + i2[0:keep])
        cand_s = jnp.concatenate(slabs_s, axis=0)
        cand_i = jnp.concatenate(slabs_i, axis=0)
        top_s, _, top_i = _top_rows(cand_s, PEER_TOPK, cand_i)
        e = jnp.exp(top_s - top_s[0:1])
        g_ref[hh * PEER_TOPK:(hh + 1) * PEER_TOPK, :] = e / jnp.sum(e, axis=0, keepdims=True)
        idx_ref[hh * PEER_TOPK:(hh + 1) * PEER_TOPK, :] = top_i


def _route(qp, keys, tm):
    n, pq = qp.shape
    return pl.pallas_call(
        _route_kernel,
        grid=(n // tm,),
        in_specs=[pl.BlockSpec((tm, pq), lambda i: (i, 0)),
                  pl.BlockSpec(keys.shape, lambda i: (0, 0, 0))],
        out_specs=[pl.BlockSpec((PEER_SEL, tm), lambda i: (0, i)), pl.BlockSpec((PEER_SEL, tm), lambda i: (0, i))],
        out_shape=[jax.ShapeDtypeStruct((PEER_SEL, n), I32), jax.ShapeDtypeStruct((PEER_SEL, n), F32)],
        compiler_params=_cparams(("arbitrary",)),
        name="route",
    )(qp, keys)


def _gelu_tanh(x):
    return 0.5 * x * (1.0 + jnp.tanh(0.7978845608028654 * (x + 0.044715 * (x * x * x))))


def _expert_kernel(tm, ec, h_ref, idx_ref, g_ref, u_ref, v_ref, o_ref, gate_ref):
    j = pl.program_id(1)
    nt = (((1,), (1,)), ((), ()))
    sub = ec // PEER_NKEYS

    @pl.when(j == 0)
    def _():
        rows = lax.broadcasted_iota(I32, (PEER_NKEYS, PEER_SEL), 0)
        grp = 8

        def tok8(t, carry):
            base = pl.multiple_of(t * grp, grp)
            ix8 = idx_ref[pl.ds(base, grp), :]
            g8 = g_ref[pl.ds(base, grp), :]
            for r in range(grp):
                ix, gg = ix8[r:r + 1, :], g8[r:r + 1, :]
                w1 = jnp.where(rows == (ix >> 7), gg, 0.0).astype(BF16)
                w2 = jnp.where(rows == (ix & (PEER_NKEYS - 1)), 1.0, 0.0).astype(BF16)
                gate_ref[pl.ds(pl.multiple_of((base + r) * GATE_PITCH, 8), PEER_NKEYS), :] = lax.dot_general(
                    w1, w2, nt, preferred_element_type=F32)
            return carry

        lax.fori_loop(0, tm // grp, tok8, 0)
        o_ref[...] = jnp.zeros(o_ref.shape, F32)

    a = jnp.dot(h_ref[...], u_ref[...], preferred_element_type=F32)
    ws = []
    for e in range(sub):
        gate = gate_ref[pl.ds(j * sub + e, tm, stride=GATE_PITCH), :]
        ws.append((gate * _gelu_tanh(a[:, e * PEER_NKEYS:(e + 1) * PEER_NKEYS])).astype(BF16))
    w = jnp.concatenate(ws, axis=-1)
    o_ref[...] += jnp.dot(w, v_ref[...], preferred_element_type=F32)


def _expert(h, idx, g, u_tab_t, v_tab, tm, ec):
    n, d = h.shape
    ne = v_tab.shape[0]
    return pl.pallas_call(
        functools.partial(_expert_kernel, tm, ec),
        grid=(n // tm, ne // ec),
        in_specs=[pl.BlockSpec((tm, d), lambda i, j: (i, 0)),
                  pl.BlockSpec((tm, PEER_SEL), lambda i, j: (i, 0)),
                  pl.BlockSpec((tm, PEER_SEL), lambda i, j: (i, 0)),
                  pl.BlockSpec((d, ec), lambda i, j: (0, j)),
                  pl.BlockSpec((ec, d), lambda i, j: (j, 0))],
        out_specs=pl.BlockSpec((tm, d), lambda i, j: (i, 0)),
        out_shape=jax.ShapeDtypeStruct((n, d), F32),
        scratch_shapes=[pltpu.VMEM((tm * GATE_PITCH, PEER_NKEYS), F32)],
        compiler_params=_cparams(("arbitrary", "arbitrary")),
        name="expert",
    )(h, idx, g, u_tab_t, v_tab)


def _final_kernel(last, x_ref, mod_ref, f_ref, gfin_ref, o_ref):
    x2 = x_ref[...] + mod_ref[...][:, 5:6, :] * f_ref[...]
    o_ref[...] = _rms(x2) * gfin_ref[...] if last else x2


def _final(x1, mod, f, g_final, bb, ts, last):
    b, s, d = x1.shape
    tok = pl.BlockSpec((bb, ts, d), lambda i, j: (i, j, 0))
    return pl.pallas_call(
        functools.partial(_final_kernel, last),
        grid=(b // bb, s // ts),
        in_specs=[tok, pl.BlockSpec((bb, 6, d), lambda i, j: (i, 0, 0)), tok,
                  pl.BlockSpec(g_final.shape, lambda i, j: (0, 0))],
        out_specs=tok,
        out_shape=jax.ShapeDtypeStruct((b, s, d), F32),
        compiler_params=_cparams(("arbitrary", "arbitrary")),
        name="final",
    )(x1, mod, f, g_final)


def _prep_weights(l, g_mix, w_in, g_q, w_uq, g_kv, w_uk, w_uv, beta_mla, beta_moba, w_o, g_ffn, w_pq,
                  sub_keys, u_tab, v_tab):
    d = w_in.shape[1]
    o_cq, o_ckv, o_kr = 0, MLA_Q_RANK, MLA_Q_RANK + MLA_KV_RANK
    o_qm = o_kr + MLA_ROPE
    o_km, o_vm = o_qm + MOBA_WIDTH, o_qm + 2 * MOBA_WIDTH
    wi = w_in[l]
    w_in_r = jnp.concatenate(
        [wi[:, o_cq:o_ckv], wi[:, o_ckv:o_kr], wi[:, o_qm:o_km], wi[:, o_km:o_vm], wi[:, o_vm:o_vm + MOBA_WIDTH],
         wi[:, o_kr:o_qm], jnp.zeros((d, LANES - MLA_ROPE), wi.dtype)], axis=1).astype(BF16)
    wq = w_uq[l].reshape(MLA_Q_RANK, MLA_HEADS, MLA_NOPE + MLA_ROPE)
    w_uq_r = jnp.concatenate([wq[:, :, :MLA_NOPE].reshape(MLA_Q_RANK, -1),
                              wq[:, :, MLA_NOPE:].reshape(MLA_Q_RANK, -1)], axis=1).astype(BF16)
    w_uk_t = jnp.transpose(w_uk[l], (1, 2, 0)).astype(BF16)
    wv = jnp.transpose(w_uv[l], (1, 0, 2))
    eye = jnp.eye(MLA_HEADS, dtype=wv.dtype)
    wuv_pad = (wv[:, :, None, :] * eye[:, None, :, None]).reshape(MLA_HEADS, MLA_KV_RANK, MLA_WIDTH).astype(BF16)
    row = lambda a: a[l].reshape(1, -1)
    return dict(
        g_mix=row(g_mix), w_in=w_in_r, g_q=row(g_q), w_uq=w_uq_r, g_kv=row(g_kv), w_uk_t=w_uk_t,
        wuv_pad=wuv_pad, beta_mla=row(beta_mla), beta_moba=row(beta_moba), w_o=w_o[l].astype(BF16),
        g_ffn=row(g_ffn), w_pq=w_pq[l].astype(BF16),
        keys=sub_keys[l].reshape(PEER_HEADS * 2, PEER_NKEYS, PEER_DKEY // 2).astype(BF16),
        u_tab_t=u_tab[l].T.astype(BF16), v_tab=v_tab[l].astype(BF16))


def _pick(n, pref):
    t = min(n, pref)
    while n % t:
        t -= 8
    return t


def kernel(x_prompt, x_sample, cache_mla_ckv, cache_mla_krope, cache_moba_k, cache_moba_v, page_table, c_prompt, c_sample, w_ada, b_ada, g_mix, w_in, g_q, w_uq, g_kv, w_uk, w_uv, beta_mla, beta_moba, w_o, g_ffn, w_pq, sub_keys, u_tab, v_tab, g_final):
    bp, sp, d = x_prompt.shape
    bs, ts, _ = x_sample.shape
    depth = w_ada.shape[0]
    n_pages = page_table.shape[1]
    past = n_pages * PAGE_SIZE
    assert sp % MOBA_BLOCK == 0 and past % MOBA_BLOCK == 0 and ts % 8 == 0
    pos_p = jnp.arange(sp, dtype=F32)
    pos_s = past + jnp.arange(ts, dtype=F32)
    n_pool = cache_moba_k.shape[1]
    gfin = g_final.reshape(1, d)

    xp, xs = x_prompt, x_sample
    new_p, new_s = [], []
    for l in range(depth):
        w = _prep_weights(l, g_mix, w_in, g_q, w_uq, g_kv, w_uk, w_uv, beta_mla, beta_moba, w_o, g_ffn, w_pq,
                          sub_keys, u_tab, v_tab)
        mod = _ada(jnp.concatenate([c_prompt, c_sample], axis=0), w_ada[l], b_ada[l])
        mod_p = mod[:bp].reshape(bp, 6, d)
        mod_s = mod[bp:].reshape(bs, 6, d)

        tp = _pick(sp, 512)
        qcat, kcat, ckv, krope, qm, kmb, vmb, km, vm, kmean = _proj(xp, mod_p, pos_p, w, 1, tp, True)
        o_mla = _mla_prompt(qcat, kcat, w["wuv_pad"], _pick(sp, 128), _pick(sp, 256))
        o_moba = _moba_prompt(qm, kmb, vmb, kmean.reshape(bp, sp // MOBA_BLOCK, MOBA_WIDTH))
        x1_p, h_p, qp_p = _oproj(xp, mod_p, o_mla, o_moba, w, 1, tp)
        new_p.append((ckv, krope, km.reshape(bp, sp, MOBA_HEADS, MOBA_HD), vm.reshape(bp, sp, MOBA_HEADS, MOBA_HD)))

        bbs = _pick(bs, 16)
        qcat_s, kcat_s, ckv_s, krope_s, qm_s, _, _, km_s, vm_s = _proj(xs, mod_s, pos_s, w, bbs, ts, False)
        kr_t = jnp.transpose(cache_mla_krope[l], (0, 2, 1))
        k_t = jnp.transpose(cache_moba_k[l], (0, 2, 3, 1)).reshape(n_pool, MOBA_WIDTH, PAGE_SIZE)
        v_t = jnp.transpose(cache_moba_v[l], (0, 2, 3, 1)).reshape(n_pool, MOBA_WIDTH, PAGE_SIZE)
        o_mla_s = _mla_sample(qcat_s, kcat_s, w["wuv_pad"], cache_mla_ckv[l], kr_t, page_table,
                              _pick(n_pages, 16))
        o_moba_s = _moba_sample(qm_s, km_s, vm_s, k_t, v_t, page_table, _pick(n_pages, 16))
        x1_s, h_s, qp_s = _oproj(xs, mod_s, o_mla_s, o_moba_s, w, bbs, ts)
        new_s.append((ckv_s, krope_s, km_s.reshape(bs, ts, MOBA_HEADS, MOBA_HD),
                      vm_s.reshape(bs, ts, MOBA_HEADS, MOBA_HD)))

        n_p, n_s = bp * sp, bs * ts
        qp = jnp.concatenate([qp_p.reshape(n_p, -1), qp_s.reshape(n_s, -1)], axis=0)
        tr = _pick(n_p + n_s, 256)
        idx_t, g_t = _route(qp, w["keys"], tr)
        h_all = jnp.concatenate([h_p.reshape(n_p, d), h_s.reshape(n_s, d)], axis=0)
        f = _expert(h_all, idx_t.T, g_t.T, w["u_tab_t"], w["v_tab"], tr, _pick(w["v_tab"].shape[0], 2048))
        last = l == depth - 1
        xp = _final(x1_p, mod_p, f[:n_p].reshape(bp, sp, d), gfin, 1, tp, last)
        xs = _final(x1_s, mod_s, f[n_p:].reshape(bs, ts, d), gfin, bbs, ts, last)

    def stack(sts, i):
        return jnp.stack([st[i] for st in sts])

    return (xp, xs, stack(new_p, 0), stack(new_p, 1), stack(new_p, 2), stack(new_p, 3),
            stack(new_s, 0), stack(new_s, 1), stack(new_s, 2), stack(new_s, 3))
```

```python
import functools

import jax
import jax.numpy as jnp
from jax import lax
from jax.experimental import pallas as pl
from jax.experimental.pallas import tpu as pltpu

F32 = jnp.float32
BF16 = jnp.bfloat16
I32 = jnp.int32

EPS = 1e-6
NEG = -1e30
ROPE_THETA = 500000.0
PAGE_SIZE = 128
MLA_HEADS = 8
MLA_NOPE = 64
MLA_ROPE = 32
MLA_V = 64
MLA_Q_RANK = 256
MLA_KV_RANK = 128
MLA_QK = MLA_KV_RANK + MLA_ROPE
MOBA_HEADS = 8
MOBA_HD = 64
MOBA_ROT = MOBA_HD // 4
MOBA_BLOCK = 256
MOBA_TOPK = 3
MOBA_WIDTH = MOBA_HEADS * MOBA_HD
MLA_WIDTH = MLA_HEADS * MLA_V
PEER_HEADS = 8
PEER_NKEYS = 128
PEER_DKEY = 256
PEER_TOPK = 16
PEER_SEL = PEER_HEADS * PEER_TOPK

LANES = 128
V7X_VMEM_BYTES = 64 * 1024 * 1024
VMEM_LIMIT = V7X_VMEM_BYTES * 7 // 8
GATE_PITCH = PEER_NKEYS + 8

_C_CQ = 0
_C_CKV = _C_CQ + MLA_Q_RANK
_C_QM = _C_CKV + MLA_KV_RANK
_C_KM = _C_QM + MOBA_WIDTH
_C_VM = _C_KM + MOBA_WIDTH
_C_KR = _C_VM + MOBA_WIDTH
_C_END = _C_KR + LANES


def _cparams(sem):
    return pltpu.CompilerParams(dimension_semantics=sem, vmem_limit_bytes=VMEM_LIMIT)


def _rms(x):
    return x * lax.rsqrt(jnp.mean(x * x, axis=-1, keepdims=True) + EPS)


def _rope(x, tab_ref, half):
    n = x.shape[-1]
    c, s1, s2 = tab_ref[0], tab_ref[1], tab_ref[2]
    return x * c + pltpu.roll(x, n - half, 1) * s1 + pltpu.roll(x, half, 1) * s2


def _ada_kernel(c_ref, w_ref, b_ref, o_ref):
    c = c_ref[...]
    a = (c * jax.nn.sigmoid(c)).astype(BF16)
    o_ref[...] = jnp.dot(a, w_ref[...].astype(BF16), preferred_element_type=F32) + b_ref[...]


def _ada(c, w, b):
    n, d = c.shape
    width = w.shape[1]
    tn = width // 4
    return pl.pallas_call(
        _ada_kernel,
        grid=(width // tn,),
        in_specs=[pl.BlockSpec((n, d), lambda j: (0, 0)),
                  pl.BlockSpec((d, tn), lambda j: (0, j)),
                  pl.BlockSpec((1, tn), lambda j: (0, j))],
        out_specs=pl.BlockSpec((n, tn), lambda j: (0, j)),
        out_shape=jax.ShapeDtypeStruct((n, width), F32),
        compiler_params=_cparams(("arbitrary",)),
        name="ada",
    )(c, w, b.reshape(1, width))


def _proj_kernel(n_mean, x_ref, mod_ref, gmix_ref, win_ref, gq_ref, wuq_ref, gkv_ref, wuk_ref,
                 tmla_ref, tkr_ref, tmob_ref,
                 qcat_ref, kcat_ref, ckv_ref, krope_ref, qm_ref, kmb_ref, vmb_ref, km_ref, vm_ref,
                 *maybe_kmean_ref):
    bb, ts, d = x_ref.shape
    m = bb * ts
    mod = mod_ref[...]
    h = _rms(x_ref[...]) * gmix_ref[...] * (1.0 + mod[:, 1:2, :]) + mod[:, 0:1, :]
    z = jnp.dot(h.reshape(m, d).astype(BF16), win_ref[...], preferred_element_type=F32)

    def rope(x, ref, half):
        if bb == 1:
            return _rope(x, ref, half)
        w = x.shape[-1]
        x3 = x.reshape(bb, ts, w)
        r1 = pltpu.roll(x, w - half, 1).reshape(bb, ts, w)
        r2 = pltpu.roll(x, half, 1).reshape(bb, ts, w)
        return (x3 * ref[0][None] + r1 * ref[1][None] + r2 * ref[2][None]).reshape(m, w)

    cq = _rms(z[:, _C_CQ:_C_CQ + MLA_Q_RANK]) * gq_ref[...]
    q = jnp.dot(cq.astype(BF16), wuq_ref[...], preferred_element_type=F32)
    scale = (MLA_NOPE + MLA_ROPE) ** -0.5
    nope_w = MLA_HEADS * MLA_NOPE
    q_rope = rope(q[:, nope_w:], tmla_ref, MLA_ROPE // 2) * scale
    for hh in range(MLA_HEADS):
        qn = q[:, hh * MLA_NOPE:(hh + 1) * MLA_NOPE].astype(BF16)
        ql = jnp.dot(qn, wuk_ref[hh], preferred_element_type=F32) * scale
        qcat_ref[:, hh, :, 0:MLA_KV_RANK] = ql.reshape(bb, ts, MLA_KV_RANK).astype(BF16)
        qr = q_rope[:, hh * MLA_ROPE:(hh + 1) * MLA_ROPE]
        qcat_ref[:, hh, :, MLA_KV_RANK:MLA_QK] = qr.reshape(bb, ts, MLA_ROPE).astype(BF16)

    ckv = _rms(z[:, _C_CKV:_C_CKV + MLA_KV_RANK]) * gkv_ref[...]
    kr = rope(z[:, _C_KR:_C_KR + LANES], tkr_ref, MLA_ROPE // 2)[:, 0:MLA_ROPE]
    ckv_ref[...] = ckv.reshape(bb, ts, MLA_KV_RANK)
    krope_ref[...] = kr.reshape(bb, ts, MLA_ROPE)
    kcat_ref[:, :, 0:MLA_KV_RANK] = ckv.reshape(bb, ts, MLA_KV_RANK).astype(BF16)
    kcat_ref[:, :, MLA_KV_RANK:MLA_QK] = kr.reshape(bb, ts, MLA_ROPE).astype(BF16)

    qm = rope(z[:, _C_QM:_C_QM + MOBA_WIDTH], tmob_ref, MOBA_ROT // 2) * (MOBA_HD ** -0.5)
    km = rope(z[:, _C_KM:_C_KM + MOBA_WIDTH], tmob_ref, MOBA_ROT // 2)
    vm = z[:, _C_VM:_C_VM + MOBA_WIDTH]
    if n_mean:
        qm_ref[0] = qm.T.astype(BF16)
        vmb_ref[0] = vm.T.astype(BF16)
    else:
        qm_ref[...] = qm.reshape(bb, ts, MOBA_WIDTH).astype(BF16)
        vmb_ref[...] = vm.reshape(bb, ts, MOBA_WIDTH).astype(BF16)
    kmb_ref[...] = km.reshape(bb, ts, MOBA_WIDTH).astype(BF16)
    km_ref[...] = km.reshape(bb, ts, MOBA_WIDTH)
    vm_ref[...] = vm.reshape(bb, ts, MOBA_WIDTH)
    if n_mean:
        kmean_ref, = maybe_kmean_ref
        for i in range(n_mean):
            blk = km[i * MOBA_BLOCK:(i + 1) * MOBA_BLOCK]
            kmean_ref[0, 0, i:i + 1, :] = jnp.sum(blk, axis=0, keepdims=True) * (1.0 / MOBA_BLOCK)


def _proj(x, mod, pos, w, bb, ts, with_kmean):
    b, s, d = x.shape
    nb, nt = b // bb, s // ts
    n_mean = ts // MOBA_BLOCK if with_kmean else 0
    tmla = _rope_tables(pos, MLA_ROPE, MLA_ROPE, MLA_HEADS)
    tkr = _rope_tables(pos, MLA_ROPE, LANES, 1)
    tmob = _rope_tables(pos, MOBA_ROT, MOBA_HD, MOBA_HEADS)

    def full(a):
        return pl.BlockSpec(a.shape, lambda i, j: (0,) * a.ndim)

    def tok(wd):
        return pl.BlockSpec((bb, ts, wd), lambda i, j: (i, j, 0))

    def tab(wd):
        return pl.BlockSpec((3, ts, wd), lambda i, j: (0, j, 0))

    out_shape = [jax.ShapeDtypeStruct((b, MLA_HEADS, s, MLA_QK), BF16),
                 jax.ShapeDtypeStruct((b, s, MLA_QK), BF16),
                 jax.ShapeDtypeStruct((b, s, MLA_KV_RANK), F32),
                 jax.ShapeDtypeStruct((b, s, MLA_ROPE), F32),
                 jax.ShapeDtypeStruct((b, s, MOBA_WIDTH), BF16),
                 jax.ShapeDtypeStruct((b, s, MOBA_WIDTH), BF16),
                 jax.ShapeDtypeStruct((b, s, MOBA_WIDTH), BF16),
                 jax.ShapeDtypeStruct((b, s, MOBA_WIDTH), F32),
                 jax.ShapeDtypeStruct((b, s, MOBA_WIDTH), F32)]
    out_specs = [pl.BlockSpec((bb, MLA_HEADS, ts, MLA_QK), lambda i, j: (i, 0, j, 0)),
                 tok(MLA_QK), tok(MLA_KV_RANK), tok(MLA_ROPE),
                 tok(MOBA_WIDTH), tok(MOBA_WIDTH), tok(MOBA_WIDTH), tok(MOBA_WIDTH), tok(MOBA_WIDTH)]
    if n_mean:
        assert bb == 1
        tr_shape = jax.ShapeDtypeStruct((b, MOBA_WIDTH, s), BF16)
        tr_spec = pl.BlockSpec((1, MOBA_WIDTH, ts), lambda i, j: (i, 0, j))
        out_shape[4], out_shape[6] = tr_shape, tr_shape
        out_specs[4], out_specs[6] = tr_spec, tr_spec
        out_shape.append(jax.ShapeDtypeStruct((b, nt, n_mean, MOBA_WIDTH), F32))
        out_specs.append(pl.BlockSpec((1, 1, n_mean, MOBA_WIDTH), lambda i, j: (i, j, 0, 0)))
    args = (x, mod, w["g_mix"], w["w_in"], w["g_q"], w["w_uq"], w["g_kv"], w["w_uk_t"], tmla, tkr, tmob)
    in_specs = [tok(d), pl.BlockSpec((bb, 6, d), lambda i, j: (i, 0, 0)),
                full(w["g_mix"]), full(w["w_in"]), full(w["g_q"]), full(w["w_uq"]), full(w["g_kv"]),
                full(w["w_uk_t"]), tab(MLA_HEADS * MLA_ROPE), tab(LANES), tab(MOBA_WIDTH)]
    return pl.pallas_call(
        functools.partial(_proj_kernel, n_mean),
        grid=(nb, nt), in_specs=in_specs, out_specs=out_specs, out_shape=out_shape,
        compiler_params=_cparams(("arbitrary", "arbitrary")),
        name="proj",
    )(*args)


def _rope_tables(pos, rot, unit, rep):
    half = rot // 2
    inv = ROPE_THETA ** (-jnp.arange(half, dtype=F32) / half)
    ang = pos[:, None] * inv[None, :]
    cos, sin = jnp.cos(ang), jnp.sin(ang)
    n = pos.shape[0]
    one, zero, zh = jnp.ones((n, unit - rot), F32), jnp.zeros((n, unit - rot), F32), jnp.zeros((n, half), F32)
    c = jnp.concatenate([cos, cos, one], axis=1)
    s1 = jnp.concatenate([-sin, zh, zero], axis=1)
    s2 = jnp.concatenate([zh, sin, zero], axis=1)
    return jnp.stack([jnp.tile(t, (1, rep)) for t in (c, s1, s2)])


def _mla_kernel(tq, tk, q_ref, k_ref, wuv_ref, o_ref):
    i = pl.program_id(1)
    rows = MLA_HEADS * tq
    q = q_ref[0].reshape(rows, MLA_QK)
    q0 = i * tq

    def step(j, carry, masked):
        m, l, acc = carry
        kb = k_ref[0, pl.ds(pl.multiple_of(j * tk, tk), tk), :]
        s = lax.dot_general(q, kb, (((1,), (1,)), ((), ())), preferred_element_type=F32)
        if masked:
            qpos = q0 + lax.broadcasted_iota(I32, (MLA_HEADS, tq, tk), 1).reshape(rows, tk)
            kpos = j * tk + lax.broadcasted_iota(I32, (rows, tk), 1)
            s = jnp.where(kpos <= qpos, s, NEG)
        m_new = jnp.maximum(m, jnp.max(s, axis=-1, keepdims=True))
        alpha = jnp.exp(m - m_new)
        p = jnp.exp(s - m_new)
        l = alpha * l + jnp.sum(p, axis=-1, keepdims=True)
        acc = alpha * acc + jnp.dot(p.astype(BF16), kb[:, 0:MLA_KV_RANK], preferred_element_type=F32)
        return m_new, l, acc

    init = (jnp.full((rows, 1), NEG, F32), jnp.zeros((rows, 1), F32), jnp.zeros((rows, MLA_KV_RANK), F32))
    nfull = q0 // tk
    carry = lax.fori_loop(0, nfull, lambda j, c: step(j, c, False), init)
    m, l, acc = step(nfull, carry, True)
    o_lat = (acc / l).astype(BF16)
    out = jnp.zeros((tq, MLA_WIDTH), F32)
    for hh in range(MLA_HEADS):
        out = out + jnp.dot(o_lat[hh * tq:(hh + 1) * tq], wuv_ref[hh], preferred_element_type=F32)
    o_ref[0] = out


def _mla_prompt(qcat, kcat, wuv_pad, tq, tk):
    b, _, s, _ = qcat.shape
    return pl.pallas_call(
        functools.partial(_mla_kernel, tq, tk),
        grid=(b, s // tq),
        in_specs=[pl.BlockSpec((1, MLA_HEADS, tq, MLA_QK), lambda i, j: (i, 0, j, 0)),
                  pl.BlockSpec((1, s, MLA_QK), lambda i, j: (i, 0, 0)),
                  pl.BlockSpec(wuv_pad.shape, lambda i, j: (0, 0, 0))],
        out_specs=pl.BlockSpec((1, tq, MLA_WIDTH), lambda i, j: (i, j, 0)),
        out_shape=jax.ShapeDtypeStruct((b, s, MLA_WIDTH), F32),
        compiler_params=_cparams(("arbitrary", "arbitrary")),
        name="mla",
    )(qcat, kcat, wuv_pad)


def _moba_kernel(nb, qt_ref, k_ref, kown_ref, vt_ref, vtown_ref, kmean_ref, o_ref, sel_ref, m_ref, l_ref, acc_ref):
    i = pl.program_id(1)
    tq = MOBA_BLOCK
    row = lax.broadcasted_iota(I32, (nb, tq), 0)

    for hh in range(MOBA_HEADS):
        lo = hh * MOBA_HD
        kmean = kmean_ref[0, :, lo:lo + MOBA_HD].astype(BF16)
        gs = jnp.dot(kmean, qt_ref[0, lo:lo + MOBA_HD, :], preferred_element_type=F32)
        gs = jnp.where(row < i, gs, NEG)
        cnt = jnp.zeros((nb, tq), F32)
        for jj in range(nb):
            cj = gs[jj:jj + 1, :]
            cnt = cnt + jnp.where(cj > gs, 1.0, 0.0) + jnp.where((cj == gs) & (jj < row), 1.0, 0.0)
        sel_ref[hh] = jnp.where((cnt < MOBA_TOPK) & (row < i), 1.0, 0.0)
        m_ref[hh] = jnp.full((1, tq), NEG, F32)
        l_ref[hh] = jnp.zeros((1, tq), F32)
        acc_ref[hh] = jnp.zeros((MOBA_HD, tq), F32)

    def attend(hh, kb, vtb, keep):
        lo = hh * MOBA_HD
        s = jnp.dot(kb, qt_ref[0, lo:lo + MOBA_HD, :], preferred_element_type=F32)
        s = jnp.where(keep, s, NEG)
        m_old = m_ref[hh]
        m_new = jnp.maximum(m_old, jnp.max(s, axis=0, keepdims=True))
        alpha = jnp.exp(m_old - m_new)
        p = jnp.where(keep, jnp.exp(s - m_new), 0.0)
        l_ref[hh] = alpha * l_ref[hh] + jnp.sum(p, axis=0, keepdims=True)
        acc_ref[hh] = alpha * acc_ref[hh] + jnp.dot(vtb, p.astype(BF16), preferred_element_type=F32)
        m_ref[hh] = m_new

    for j in range(nb - 1):
        @pl.when(j < i)
        def _(j=j):
            for hh in range(MOBA_HEADS):
                lo = hh * MOBA_HD
                attend(hh, k_ref[0, j * tq:(j + 1) * tq, lo:lo + MOBA_HD],
                       vt_ref[0, lo:lo + MOBA_HD, j * tq:(j + 1) * tq], sel_ref[hh][j:j + 1, :] > 0.5)

    causal = lax.broadcasted_iota(I32, (tq, tq), 0) <= lax.broadcasted_iota(I32, (tq, tq), 1)
    for hh in range(MOBA_HEADS):
        lo = hh * MOBA_HD
        attend(hh, kown_ref[0, :, lo:lo + MOBA_HD], vtown_ref[0, lo:lo + MOBA_HD, :], causal)
    ot = jnp.concatenate([acc_ref[hh] / l_ref[hh] for hh in range(MOBA_HEADS)], axis=0)
    o_ref[0] = ot.T


def _moba_prompt(qmt, kmb, vmt, kmean):
    b, _, s = qmt.shape
    nb = s // MOBA_BLOCK
    return pl.pallas_call(
        functools.partial(_moba_kernel, nb),
        grid=(b, nb),
        in_specs=[pl.BlockSpec((1, MOBA_WIDTH, MOBA_BLOCK), lambda i, j: (i, 0, j)),
                  pl.BlockSpec((1, s, MOBA_WIDTH), lambda i, j: (i, 0, 0)),
                  pl.BlockSpec((1, MOBA_BLOCK, MOBA_WIDTH), lambda i, j: (i, j, 0)),
                  pl.BlockSpec((1, MOBA_WIDTH, s), lambda i, j: (i, 0, 0)),
                  pl.BlockSpec((1, MOBA_WIDTH, MOBA_BLOCK), lambda i, j: (i, 0, j)),
                  pl.BlockSpec((1, nb, MOBA_WIDTH), lambda i, j: (i, 0, 0))],
        out_specs=pl.BlockSpec((1, MOBA_BLOCK, MOBA_WIDTH), lambda i, j: (i, j, 0)),
        out_shape=jax.ShapeDtypeStruct((b, s, MOBA_WIDTH), F32),
        scratch_shapes=[pltpu.VMEM((MOBA_HEADS, nb, MOBA_BLOCK), F32),
                        pltpu.VMEM((MOBA_HEADS, 1, MOBA_BLOCK), F32),
                        pltpu.VMEM((MOBA_HEADS, 1, MOBA_BLOCK), F32),
                        pltpu.VMEM((MOBA_HEADS, MOBA_HD, MOBA_BLOCK), F32)],
        compiler_params=_cparams(("arbitrary", "arbitrary")),
        name="moba",
    )(qmt, kmb, kmb, vmt, vmt, kmean)


def _mla_dec_kernel(pps, t_new, pt_ref, q_ref, knew_ref, wuv_ref, *rest):
    ckv_refs, kr_refs = rest[:pps], rest[pps:2 * pps]
    o_ref, m_ref, l_ref, acc_ref = rest[2 * pps:]
    c = pl.program_id(1)
    rows = MLA_HEADS * t_new
    q = q_ref[0].reshape(rows, MLA_QK)
    ql, qr = q[:, 0:MLA_KV_RANK], q[:, MLA_KV_RANK:MLA_QK]

    @pl.when(c == 0)
    def _():
        m_ref[...] = jnp.full(m_ref.shape, NEG, F32)
        l_ref[...] = jnp.zeros(l_ref.shape, F32)
        acc_ref[...] = jnp.zeros(acc_ref.shape, F32)

    nt = (((1,), (1,)), ((), ()))
    kc = jnp.concatenate([r[0].astype(BF16) for r in ckv_refs], axis=0)
    krt = jnp.concatenate([r[0].astype(BF16) for r in kr_refs], axis=1)
    s = (lax.dot_general(ql, kc, nt, preferred_element_type=F32)
         + jnp.dot(qr, krt, preferred_element_type=F32))
    m_old = m_ref[...]
    m_new = jnp.maximum(m_old, jnp.max(s, axis=-1, keepdims=True))
    alpha = jnp.exp(m_old - m_new)
    p = jnp.exp(s - m_new)
    l_ref[...] = alpha * l_ref[...] + jnp.sum(p, axis=-1, keepdims=True)
    acc_ref[...] = alpha * acc_ref[...] + jnp.dot(p.astype(BF16), kc, preferred_element_type=F32)
    m_ref[...] = m_new

    @pl.when(c == pl.num_programs(1) - 1)
    def _():
        kn = knew_ref[0]
        sn = lax.dot_general(q, kn, nt, preferred_element_type=F32)
        tq = lax.broadcasted_iota(I32, (MLA_HEADS, t_new, t_new), 1).reshape(rows, t_new)
        tk = lax.broadcasted_iota(I32, (rows, t_new), 1)
        sn = jnp.where(tk <= tq, sn, NEG)
        m1 = m_ref[...]
        m2 = jnp.maximum(m1, jnp.max(sn, axis=-1, keepdims=True))
        a2 = jnp.exp(m1 - m2)
        pn = jnp.exp(sn - m2)
        l2 = a2 * l_ref[...] + jnp.sum(pn, axis=-1, keepdims=True)
        acc2 = a2 * acc_ref[...] + jnp.dot(pn.astype(BF16), kn[:, 0:MLA_KV_RANK], preferred_element_type=F32)
        o_lat = (acc2 / l2).astype(BF16)
        out = jnp.zeros((t_new, MLA_WIDTH), F32)
        for hh in range(MLA_HEADS):
            out = out + jnp.dot(o_lat[hh * t_new:(hh + 1) * t_new], wuv_ref[hh], preferred_element_type=F32)
        o_ref[0] = out


def _mla_sample(qcat, kcat_new, wuv_pad, cache_ckv, cache_kr, page_table, pps):
    b, _, t_new, _ = qcat.shape
    n_pages = page_table.shape[1]
    rows = MLA_HEADS * t_new

    def page_spec(k, shape):
        return pl.BlockSpec((1,) + shape, lambda i, c, pt: (pt[i, c * pps + k], 0, 0))

    in_specs = ([pl.BlockSpec((1, MLA_HEADS, t_new, MLA_QK), lambda i, c, pt: (i, 0, 0, 0)),
                 pl.BlockSpec((1, t_new, MLA_QK), lambda i, c, pt: (i, 0, 0)),
                 pl.BlockSpec(wuv_pad.shape, lambda i, c, pt: (0, 0, 0))]
                + [page_spec(k, (PAGE_SIZE, MLA_KV_RANK)) for k in range(pps)]
                + [page_spec(k, (MLA_ROPE, PAGE_SIZE)) for k in range(pps)])
    grid_spec = pltpu.PrefetchScalarGridSpec(
        num_scalar_prefetch=1, grid=(b, n_pages // pps), in_specs=in_specs,
        out_specs=pl.BlockSpec((1, t_new, MLA_WIDTH), lambda i, c, pt: (i, 0, 0)),
        scratch_shapes=[pltpu.VMEM((rows, 1), F32), pltpu.VMEM((rows, 1), F32),
                        pltpu.VMEM((rows, MLA_KV_RANK), F32)])
    return pl.pallas_call(
        functools.partial(_mla_dec_kernel, pps, t_new),
        grid_spec=grid_spec,
        out_shape=jax.ShapeDtypeStruct((b, t_new, MLA_WIDTH), F32),
        compiler_params=_cparams(("arbitrary", "arbitrary")),
        name="mla_dec",
    )(page_table, qcat, kcat_new, wuv_pad, *([cache_ckv] * pps), *([cache_kr] * pps))


def _moba_dec_kernel(pps, t_new, n_pages, pt_ref, q_ref, knew_ref, vnew_ref, *rest):
    k_refs, v_refs = rest[:pps], rest[pps:2 * pps]
    o_ref, qbd_ref, s_ref, p_ref, gs_ref, l_ref, own_ref, acc_ref = rest[2 * pps:]
    phase, c = pl.program_id(1), pl.program_id(2)
    nc = n_pages // pps
    rows = MOBA_HEADS * t_new
    n_blk = n_pages * PAGE_SIZE // MOBA_BLOCK
    bps = pps * PAGE_SIZE // MOBA_BLOCK
    nt = (((1,), (1,)), ((), ()))
    bcol = lax.broadcasted_iota(I32, (rows, n_blk), 1)

    @pl.when((phase == 0) & (c == 0))
    def _():
        qt = jnp.concatenate([q_ref[0].astype(F32)] * MOBA_HEADS, axis=0)
        rh = lax.broadcasted_iota(I32, (MOBA_HEADS, t_new, MOBA_WIDTH), 0).reshape(rows, MOBA_WIDTH)
        ch = lax.broadcasted_iota(I32, (rows, MOBA_HEADS, MOBA_HD), 1).reshape(rows, MOBA_WIDTH)
        qbd_ref[...] = jnp.where(rh == ch, qt, 0.0).astype(BF16)
        gs_ref[...] = jnp.zeros(gs_ref.shape, F32)

    @pl.when(phase == 0)
    def _():
        kt = jnp.concatenate([r[0].astype(BF16) for r in k_refs], axis=1)
        s = jnp.dot(qbd_ref[...], kt, preferred_element_type=F32)
        s_ref[c] = s
        gs = gs_ref[...]
        for kb in range(bps):
            gsum = jnp.sum(s[:, kb * MOBA_BLOCK:(kb + 1) * MOBA_BLOCK], axis=-1, keepdims=True)
            gs = jnp.where(bcol == c * bps + kb, gsum, gs)
        gs_ref[...] = gs

    @pl.when((phase == 0) & (c == nc - 1))
    def _():
        qbd = qbd_ref[...]
        gs = gs_ref[...]
        sel = jnp.zeros((rows, n_blk), F32)
        for _ in range(min(MOBA_TOPK, n_blk)):
            mx = jnp.max(gs, axis=-1, keepdims=True)
            first = jnp.min(jnp.where(gs == mx, bcol, n_blk), axis=-1, keepdims=True)
            hit = bcol == first
            sel = jnp.where(hit, 1.0, sel)
            gs = jnp.where(hit, -jnp.inf, gs)
        kn = knew_ref[0].astype(BF16)
        so = lax.dot_general(qbd, kn, nt, preferred_element_type=F32)
        tq = lax.broadcasted_iota(I32, (MOBA_HEADS, t_new, t_new), 1).reshape(rows, t_new)
        tk = lax.broadcasted_iota(I32, (rows, t_new), 1)
        so = jnp.where(tk <= tq, so, NEG)

        def block_scores(blk):
            lo = (blk % bps) * MOBA_BLOCK
            return s_ref[blk // bps, :, lo:lo + MOBA_BLOCK], sel[:, blk:blk + 1] > 0.5

        half = MOBA_BLOCK // 2
        m_el = jnp.full((rows, half), NEG, F32)
        for blk in range(n_blk):
            sb, keep = block_scores(blk)
            m_el = jnp.maximum(m_el, jnp.where(keep, jnp.maximum(sb[:, :half], sb[:, half:]), NEG))
        m = jnp.maximum(jnp.max(m_el, axis=-1, keepdims=True), jnp.max(so, axis=-1, keepdims=True))
        l_el = jnp.zeros((rows, half), F32)
        for blk in range(n_blk):
            sb, keep = block_scores(blk)
            p = jnp.where(keep, jnp.exp(jnp.where(keep, sb, NEG) - m), 0.0)
            l_el = l_el + p[:, :half] + p[:, half:]
            lo = (blk % bps) * MOBA_BLOCK
            p_ref[blk // bps, :, lo:lo + MOBA_BLOCK] = p.astype(BF16)
        po = jnp.exp(so - m)
        l_ref[...] = jnp.sum(l_el, axis=-1, keepdims=True) + jnp.sum(po, axis=-1, keepdims=True)
        own_ref[...] = jnp.dot(po.astype(BF16), vnew_ref[0].astype(BF16), preferred_element_type=F32)
        acc_ref[...] = jnp.zeros(acc_ref.shape, F32)

    @pl.when(phase == 1)
    def _():
        vt = jnp.concatenate([r[0].astype(BF16) for r in v_refs], axis=1)
        acc_ref[...] += lax.dot_general(p_ref[c], vt, nt, preferred_element_type=F32)

    @pl.when((phase == 1) & (c == nc - 1))
    def _():
        full = (acc_ref[...] + own_ref[...]) / l_ref[...]
        for hh in range(MOBA_HEADS):
            o_ref[0, :, hh * MOBA_HD:(hh + 1) * MOBA_HD] = full[hh * t_new:(hh + 1) * t_new,
                                                                 hh * MOBA_HD:(hh + 1) * MOBA_HD]


def _moba_sample(qm, km_new, vm_new, cache_k, cache_v, page_table, pps):
    b, t_new, _ = qm.shape
    n_pages = page_table.shape[1]
    nc = n_pages // pps
    rows = MOBA_HEADS * t_new
    n_blk = n_pages * PAGE_SIZE // MOBA_BLOCK

    def k_spec(k):
        return pl.BlockSpec((1, MOBA_WIDTH, PAGE_SIZE),
                            lambda i, ph, c, pt: (pt[i, jnp.where(ph == 0, c, nc - 1) * pps + k], 0, 0))

    def v_spec(k):
        return pl.BlockSpec((1, MOBA_WIDTH, PAGE_SIZE),
                            lambda i, ph, c, pt: (pt[i, jnp.where(ph == 0, 0, c) * pps + k], 0, 0))

    tok = pl.BlockSpec((1, t_new, MOBA_WIDTH), lambda i, ph, c, pt: (i, 0, 0))
    grid_spec = pltpu.PrefetchScalarGridSpec(
        num_scalar_prefetch=1, grid=(b, 2, nc),
        in_specs=[tok, tok, tok] + [k_spec(k) for k in range(pps)] + [v_spec(k) for k in range(pps)],
        out_specs=tok,
        scratch_shapes=[pltpu.VMEM((rows, MOBA_WIDTH), BF16),
                        pltpu.VMEM((nc, rows, pps * PAGE_SIZE), F32),
                        pltpu.VMEM((nc, rows, pps * PAGE_SIZE), BF16),
                        pltpu.VMEM((rows, n_blk), F32),
                        pltpu.VMEM((rows, 1), F32),
                        pltpu.VMEM((rows, MOBA_WIDTH), F32),
                        pltpu.VMEM((rows, MOBA_WIDTH), F32)])
    return pl.pallas_call(
        functools.partial(_moba_dec_kernel, pps, t_new, n_pages),
        grid_spec=grid_spec,
        out_shape=jax.ShapeDtypeStruct((b, t_new, MOBA_WIDTH), F32),
        compiler_params=_cparams(("arbitrary", "arbitrary", "arbitrary")),
        name="moba_dec",
    )(page_table, qm, km_new, vm_new, *([cache_k] * pps), *([cache_v] * pps))


def _oproj_kernel(x_ref, mod_ref, omla_ref, omoba_ref, bmla_ref, bmoba_ref, wo_ref, gffn_ref, wpq_ref,
                  x1_ref, h_ref, qp_ref):
    bb, ts, d = x_ref.shape
    m = bb * ts
    mod = mod_ref[...]
    a = (_rms(omla_ref[...]) * bmla_ref[...]).reshape(m, MLA_WIDTH).astype(BF16)
    b = (_rms(omoba_ref[...]) * bmoba_ref[...]).reshape(m, MOBA_WIDTH).astype(BF16)
    o = jnp.concatenate([a, b], axis=-1)
    y = jnp.dot(o, wo_ref[...], preferred_element_type=F32).reshape(bb, ts, d)
    x1 = x_ref[...] + mod[:, 2:3, :] * y
    x1_ref[...] = x1
    h = (_rms(x1) * gffn_ref[...] * (1.0 + mod[:, 4:5, :]) + mod[:, 3:4, :]).astype(BF16)
    h_ref[...] = h
    qp = jnp.dot(h.reshape(m, d), wpq_ref[...], preferred_element_type=F32)
    qp_ref[...] = qp.reshape(bb, ts, qp.shape[-1]).astype(BF16)


def _oproj(x, mod, o_mla, o_moba, w, bb, ts):
    b, s, d = x.shape
    pq = w["w_pq"].shape[1]

    def full(a):
        return pl.BlockSpec(a.shape, lambda i, j: (0,) * a.ndim)

    def tok(wd):
        return pl.BlockSpec((bb, ts, wd), lambda i, j: (i, j, 0))

    return pl.pallas_call(
        _oproj_kernel,
        grid=(b // bb, s // ts),
        in_specs=[tok(d), pl.BlockSpec((bb, 6, d), lambda i, j: (i, 0, 0)), tok(MLA_WIDTH), tok(MOBA_WIDTH),
                  full(w["beta_mla"]), full(w["beta_moba"]), full(w["w_o"]), full(w["g_ffn"]), full(w["w_pq"])],
        out_specs=[tok(d), tok(d), tok(pq)],
        out_shape=[jax.ShapeDtypeStruct((b, s, d), F32), jax.ShapeDtypeStruct((b, s, d), BF16),
                   jax.ShapeDtypeStruct((b, s, pq), BF16)],
        compiler_params=_cparams(("arbitrary", "arbitrary")),
        name="oproj",
    )(x, mod, o_mla, o_moba, w["beta_mla"], w["beta_moba"], w["w_o"], w["g_ffn"], w["w_pq"])


def _top_rows(s, k, payload=None):
    n = s.shape[0]
    row = lax.broadcasted_iota(I32, s.shape, 0)
    vals, idxs, picks = [], [], []
    for _ in range(k):
        mx = jnp.max(s, axis=0, keepdims=True)
        first = jnp.min(jnp.where(s == mx, row, n), axis=0, keepdims=True)
        hit = row == first
        vals.append(mx)
        idxs.append(first)
        if payload is not None:
            picks.append(jnp.max(jnp.where(hit, payload, -1), axis=0, keepdims=True))
        s = jnp.where(hit, -jnp.inf, s)
    cat = lambda xs: jnp.concatenate(xs, axis=0)
    return cat(vals), cat(idxs), (cat(picks) if payload is not None else None)


def _route_kernel(qp_ref, keys_ref, idx_ref, g_ref):
    nt = (((1,), (1,)), ((), ()))
    half = PEER_DKEY // 2
    for hh in range(PEER_HEADS):
        tops = []
        for p in range(2):
            lo = (hh * 2 + p) * half
            s = lax.dot_general(keys_ref[hh * 2 + p], qp_ref[:, lo:lo + half], nt,
                                preferred_element_type=F32)
            v, ix, _ = _top_rows(s, PEER_TOPK)
            tops.append((v, ix))
        (s1, i1), (s2, i2) = tops
        slabs_s, slabs_i = [], []
        for a in range(PEER_TOPK):
            nb = PEER_TOPK // (a + 1)
            keep = PEER_TOPK if nb > 8 else 8
            ss = s1[a:a + 1] + s2[0:keep]
            if nb < keep:
                ss = jnp.where(lax.broadcasted_iota(I32, ss.shape, 0) < nb, ss, -jnp.inf)
            slabs_s.append(ss)
            slabs_i.append(i1[a:a + 1] * PEER_NKEYS + i2[0:keep])
        cand_s = jnp.concatenate(slabs_s, axis=0)
        cand_i = jnp.concatenate(slabs_i, axis=0)
        top_s, _, top_i = _top_rows(cand_s, PEER_TOPK, cand_i)
        e = jnp.exp(top_s - top_s[0:1])
        g_ref[hh * PEER_TOPK:(hh + 1) * PEER_TOPK, :] = e / jnp.sum(e, axis=0, keepdims=True)
        idx_ref[hh * PEER_TOPK:(hh + 1) * PEER_TOPK, :] = top_i


def _route(qp, keys, tm):
    n, pq = qp.shape
    return pl.pallas_call(
        _route_kernel,
        grid=(n // tm,),
        in_specs=[pl.BlockSpec((tm, pq), lambda i: (i, 0)),
                  pl.BlockSpec(keys.shape, lambda i: (0, 0, 0))],
        out_specs=[pl.BlockSpec((PEER_SEL, tm), lambda i: (0, i)), pl.BlockSpec((PEER_SEL, tm), lambda i: (0, i))],
        out_shape=[jax.ShapeDtypeStruct((PEER_SEL, n), I32), jax.ShapeDtypeStruct((PEER_SEL, n), F32)],
        compiler_params=_cparams(("arbitrary",)),
        name="route",
    )(qp, keys)


def _gelu_tanh(x):
    return 0.5 * x * (1.0 + jnp.tanh(0.7978845608028654 * (x + 0.044715 * (x * x * x))))


def _expert_kernel(tm, ec, h_ref, idx_ref, g_ref, u_ref, v_ref, o_ref, gate_ref):
    j = pl.program_id(1)
    nt = (((1,), (1,)), ((), ()))
    sub = ec // PEER_NKEYS

    @pl.when(j == 0)
    def _():
        rows = lax.broadcasted_iota(I32, (PEER_NKEYS, PEER_SEL), 0)
        grp = 16 if tm % 16 == 0 else 8

        def tok8(t, carry):
            base = pl.multiple_of(t * grp, grp)
            ix8 = idx_ref[pl.ds(base, grp), :]
            g8 = g_ref[pl.ds(base, grp), :]
            for r in range(grp):
                ix, gg = ix8[r:r + 1, :], g8[r:r + 1, :]
                w1 = jnp.where(rows == (ix >> 7), gg, 0.0).astype(BF16)
                w2 = jnp.where(rows == (ix & (PEER_NKEYS - 1)), 1.0, 0.0).astype(BF16)
                gate_ref[pl.ds(pl.multiple_of((base + r) * GATE_PITCH, 8), PEER_NKEYS), :] = lax.dot_general(
                    w1, w2, nt, preferred_element_type=F32)
            return carry

        lax.fori_loop(0, tm // grp, tok8, 0)
        o_ref[...] = jnp.zeros(o_ref.shape, F32)

    a = jnp.dot(h_ref[...], u_ref[...], preferred_element_type=F32)
    ws = []
    for e in range(sub):
        gate = gate_ref[pl.ds(j * sub + e, tm, stride=GATE_PITCH), :]
        ws.append((gate * _gelu_tanh(a[:, e * PEER_NKEYS:(e + 1) * PEER_NKEYS])).astype(BF16))
    w = jnp.concatenate(ws, axis=-1)
    o_ref[...] += jnp.dot(w, v_ref[...], preferred_element_type=F32)


def _expert(h, idx, g, u_tab_t, v_tab, tm, ec):
    n, d = h.shape
    ne = v_tab.shape[0]
    return pl.pallas_call(
        functools.partial(_expert_kernel, tm, ec),
        grid=(n // tm, ne // ec),
        in_specs=[pl.BlockSpec((tm, d), lambda i, j: (i, 0)),
                  pl.BlockSpec((tm, PEER_SEL), lambda i, j: (i, 0)),
                  pl.BlockSpec((tm, PEER_SEL), lambda i, j: (i, 0)),
                  pl.BlockSpec((d, ec), lambda i, j: (0, j)),
                  pl.BlockSpec((ec, d), lambda i, j: (j, 0))],
        out_specs=pl.BlockSpec((tm, d), lambda i, j: (i, 0)),
        out_shape=jax.ShapeDtypeStruct((n, d), F32),
        scratch_shapes=[pltpu.VMEM((tm * GATE_PITCH, PEER_NKEYS), F32)],
        compiler_params=_cparams(("arbitrary", "arbitrary")),
        name="expert",
    )(h, idx, g, u_tab_t, v_tab)


def _final_kernel(last, x_ref, mod_ref, f_ref, gfin_ref, o_ref):
    x2 = x_ref[...] + mod_ref[...][:, 5:6, :] * f_ref[...]
    o_ref[...] = _rms(x2) * gfin_ref[...] if last else x2


def _final(x1, mod, f, g_final, bb, ts, last):
    b, s, d = x1.shape
    tok = pl.BlockSpec((bb, ts, d), lambda i, j: (i, j, 0))
    return pl.pallas_call(
        functools.partial(_final_kernel, last),
        grid=(b // bb, s // ts),
        in_specs=[tok, pl.BlockSpec((bb, 6, d), lambda i, j: (i, 0, 0)), tok,
                  pl.BlockSpec(g_final.shape, lambda i, j: (0, 0))],
        out_specs=tok,
        out_shape=jax.ShapeDtypeStruct((b, s, d), F32),
        compiler_params=_cparams(("arbitrary", "arbitrary")),
        name="final",
    )(x1, mod, f, g_final)


def _prep_weights(l, g_mix, w_in, g_q, w_uq, g_kv, w_uk, w_uv, beta_mla, beta_moba, w_o, g_ffn, w_pq,
                  sub_keys, u_tab, v_tab):
    d = w_in.shape[1]
    o_cq, o_ckv, o_kr = 0, MLA_Q_RANK, MLA_Q_RANK + MLA_KV_RANK
    o_qm = o_kr + MLA_ROPE
    o_km, o_vm = o_qm + MOBA_WIDTH, o_qm + 2 * MOBA_WIDTH
    wi = w_in[l]
    w_in_r = jnp.concatenate(
        [wi[:, o_cq:o_ckv], wi[:, o_ckv:o_kr], wi[:, o_qm:o_km], wi[:, o_km:o_vm], wi[:, o_vm:o_vm + MOBA_WIDTH],
         wi[:, o_kr:o_qm], jnp.zeros((d, LANES - MLA_ROPE), wi.dtype)], axis=1).astype(BF16)
    wq = w_uq[l].reshape(MLA_Q_RANK, MLA_HEADS, MLA_NOPE + MLA_ROPE)
    w_uq_r = jnp.concatenate([wq[:, :, :MLA_NOPE].reshape(MLA_Q_RANK, -1),
                              wq[:, :, MLA_NOPE:].reshape(MLA_Q_RANK, -1)], axis=1).astype(BF16)
    w_uk_t = jnp.transpose(w_uk[l], (1, 2, 0)).astype(BF16)
    wv = jnp.transpose(w_uv[l], (1, 0, 2))
    eye = jnp.eye(MLA_HEADS, dtype=wv.dtype)
    wuv_pad = (wv[:, :, None, :] * eye[:, None, :, None]).reshape(MLA_HEADS, MLA_KV_RANK, MLA_WIDTH).astype(BF16)
    row = lambda a: a[l].reshape(1, -1)
    return dict(
        g_mix=row(g_mix), w_in=w_in_r, g_q=row(g_q), w_uq=w_uq_r, g_kv=row(g_kv), w_uk_t=w_uk_t,
        wuv_pad=wuv_pad, beta_mla=row(beta_mla), beta_moba=row(beta_moba), w_o=w_o[l].astype(BF16),
        g_ffn=row(g_ffn), w_pq=w_pq[l].astype(BF16),
        keys=sub_keys[l].reshape(PEER_HEADS * 2, PEER_NKEYS, PEER_DKEY // 2).astype(BF16),
        u_tab_t=u_tab[l].T.astype(BF16), v_tab=v_tab[l].astype(BF16))


def _pick(n, pref):
    t = min(n, pref)
    while n % t:
        t -= 8
    return t


def kernel(x_prompt, x_sample, cache_mla_ckv, cache_mla_krope, cache_moba_k, cache_moba_v, page_table, c_prompt, c_sample, w_ada, b_ada, g_mix, w_in, g_q, w_uq, g_kv, w_uk, w_uv, beta_mla, beta_moba, w_o, g_ffn, w_pq, sub_keys, u_tab, v_tab, g_final):
    bp, sp, d = x_prompt.shape
    bs, ts, _ = x_sample.shape
    depth = w_ada.shape[0]
    n_pages = page_table.shape[1]
    past = n_pages * PAGE_SIZE
    assert sp % MOBA_BLOCK == 0 and past % MOBA_BLOCK == 0 and ts % 8 == 0
    pos_p = jnp.arange(sp, dtype=F32)
    pos_s = past + jnp.arange(ts, dtype=F32)
    n_pool = cache_moba_k.shape[1]
    gfin = g_final.reshape(1, d)

    xp, xs = x_prompt, x_sample
    new_p, new_s = [], []
    for l in range(depth):
        w = _prep_weights(l, g_mix, w_in, g_q, w_uq, g_kv, w_uk, w_uv, beta_mla, beta_moba, w_o, g_ffn, w_pq,
                          sub_keys, u_tab, v_tab)
        mod = _ada(jnp.concatenate([c_prompt, c_sample], axis=0), w_ada[l], b_ada[l])
        mod_p = mod[:bp].reshape(bp, 6, d)
        mod_s = mod[bp:].reshape(bs, 6, d)

        tp = _pick(sp, 512)
        qcat, kcat, ckv, krope, qm, kmb, vmb, km, vm, kmean = _proj(xp, mod_p, pos_p, w, 1, tp, True)
        o_mla = _mla_prompt(qcat, kcat, w["wuv_pad"], _pick(sp, 128), _pick(sp, 256))
        o_moba = _moba_prompt(qm, kmb, vmb, kmean.reshape(bp, sp // MOBA_BLOCK, MOBA_WIDTH))
        x1_p, h_p, qp_p = _oproj(xp, mod_p, o_mla, o_moba, w, 1, tp)
        new_p.append((ckv, krope, km.reshape(bp, sp, MOBA_HEADS, MOBA_HD), vm.reshape(bp, sp, MOBA_HEADS, MOBA_HD)))

        bbs = _pick(bs, 16)
        qcat_s, kcat_s, ckv_s, krope_s, qm_s, _, _, km_s, vm_s = _proj(xs, mod_s, pos_s, w, bbs, ts, False)
        kr_t = jnp.transpose(cache_mla_krope[l], (0, 2, 1))
        k_t = jnp.transpose(cache_moba_k[l], (0, 2, 3, 1)).reshape(n_pool, MOBA_WIDTH, PAGE_SIZE)
        v_t = jnp.transpose(cache_moba_v[l], (0, 2, 3, 1)).reshape(n_pool, MOBA_WIDTH, PAGE_SIZE)
        o_mla_s = _mla_sample(qcat_s, kcat_s, w["wuv_pad"], cache_mla_ckv[l], kr_t, page_table,
                              _pick(n_pages, 32))
        o_moba_s = _moba_sample(qm_s, km_s, vm_s, k_t, v_t, page_table, _pick(n_pages, 32))
        x1_s, h_s, qp_s = _oproj(xs, mod_s, o_mla_s, o_moba_s, w, bbs, ts)
        new_s.append((ckv_s, krope_s, km_s.reshape(bs, ts, MOBA_HEADS, MOBA_HD),
                      vm_s.reshape(bs, ts, MOBA_HEADS, MOBA_HD)))

        n_p, n_s = bp * sp, bs * ts
        qp = jnp.concatenate([qp_p.reshape(n_p, -1), qp_s.reshape(n_s, -1)], axis=0)
        tr = _pick(n_p + n_s, 256)
        idx_t, g_t = _route(qp, w["keys"], tr)
        h_all = jnp.concatenate([h_p.reshape(n_p, d), h_s.reshape(n_s, d)], axis=0)
        f = _expert(h_all, idx_t.T, g_t.T, w["u_tab_t"], w["v_tab"], tr, _pick(w["v_tab"].shape[0], 2048))
        last = l == depth - 1
        xp = _final(x1_p, mod_p, f[:n_p].reshape(bp, sp, d), gfin, 1, tp, last)
        xs = _final(x1_s, mod_s, f[n_p:].reshape(bs, ts, d), gfin, bbs, ts, last)

    def stack(sts, i):
        return jnp.stack([st[i] for st in sts])

    return (xp, xs, stack(new_p, 0), stack(new_p, 1), stack(new_p, 2), stack(new_p, 3),
            stack(new_s, 0), stack(new_s, 1), stack(new_s, 2), stack(new_s, 3))
```

```python
import functools

import jax
import jax.numpy as jnp
from jax import lax
from jax.experimental import pallas as pl
from jax.experimental.pallas import tpu as pltpu

F32 = jnp.float32
BF16 = jnp.bfloat16
I32 = jnp.int32

EPS = 1e-6
NEG = -1e30
ROPE_THETA = 500000.0
PAGE_SIZE = 128
MLA_HEADS = 8
MLA_NOPE = 64
MLA_ROPE = 32
MLA_V = 64
MLA_Q_RANK = 256
MLA_KV_RANK = 128
MLA_QK = MLA_KV_RANK + MLA_ROPE
MOBA_HEADS = 8
MOBA_HD = 64
MOBA_ROT = MOBA_HD // 4
MOBA_BLOCK = 256
MOBA_TOPK = 3
MOBA_WIDTH = MOBA_HEADS * MOBA_HD
MLA_WIDTH = MLA_HEADS * MLA_V
PEER_HEADS = 8
PEER_NKEYS = 128
PEER_DKEY = 256
PEER_TOPK = 16
PEER_SEL = PEER_HEADS * PEER_TOPK

LANES = 128
V7X_VMEM_BYTES = 64 * 1024 * 1024
VMEM_LIMIT = V7X_VMEM_BYTES * 7 // 8
GATE_PITCH = PEER_NKEYS + 8

_C_CQ = 0
_C_CKV = _C_CQ + MLA_Q_RANK
_C_QM = _C_CKV + MLA_KV_RANK
_C_KM = _C_QM + MOBA_WIDTH
_C_VM = _C_KM + MOBA_WIDTH
_C_KR = _C_VM + MOBA_WIDTH
_C_END = _C_KR + LANES


def _cparams(sem):
    return pltpu.CompilerParams(dimension_semantics=sem, vmem_limit_bytes=VMEM_LIMIT)


def _rms(x):
    return x * lax.rsqrt(jnp.mean(x * x, axis=-1, keepdims=True) + EPS)


def _rope(x, tab_ref, half):
    n = x.shape[-1]
    c, s1, s2 = tab_ref[0], tab_ref[1], tab_ref[2]
    return x * c + pltpu.roll(x, n - half, 1) * s1 + pltpu.roll(x, half, 1) * s2


def _ada_kernel(c_ref, w_ref, b_ref, o_ref):
    c = c_ref[...]
    a = (c * jax.nn.sigmoid(c)).astype(BF16)
    o_ref[...] = jnp.dot(a, w_ref[...].astype(BF16), preferred_element_type=F32) + b_ref[...]


def _ada(c, w, b):
    n, d = c.shape
    width = w.shape[1]
    tn = width // 4
    return pl.pallas_call(
        _ada_kernel,
        grid=(width // tn,),
        in_specs=[pl.BlockSpec((n, d), lambda j: (0, 0)),
                  pl.BlockSpec((d, tn), lambda j: (0, j)),
                  pl.BlockSpec((1, tn), lambda j: (0, j))],
        out_specs=pl.BlockSpec((n, tn), lambda j: (0, j)),
        out_shape=jax.ShapeDtypeStruct((n, width), F32),
        compiler_params=_cparams(("arbitrary",)),
        name="ada",
    )(c, w, b.reshape(1, width))


def _proj_kernel(n_mean, x_ref, mod_ref, gmix_ref, win_ref, gq_ref, wuq_ref, gkv_ref, wuk_ref,
                 tmla_ref, tkr_ref, tmob_ref,
                 qcat_ref, kcat_ref, ckv_ref, krope_ref, qm_ref, kmb_ref, vmb_ref, km_ref, vm_ref,
                 *maybe_kmean_ref):
    bb, ts, d = x_ref.shape
    m = bb * ts
    mod = mod_ref[...]
    h = _rms(x_ref[...]) * gmix_ref[...] * (1.0 + mod[:, 1:2, :]) + mod[:, 0:1, :]
    z = jnp.dot(h.reshape(m, d).astype(BF16), win_ref[...], preferred_element_type=F32)

    def rope(x, ref, half):
        if bb == 1:
            return _rope(x, ref, half)
        w = x.shape[-1]
        x3 = x.reshape(bb, ts, w)
        r1 = pltpu.roll(x, w - half, 1).reshape(bb, ts, w)
        r2 = pltpu.roll(x, half, 1).reshape(bb, ts, w)
        return (x3 * ref[0][None] + r1 * ref[1][None] + r2 * ref[2][None]).reshape(m, w)

    cq = _rms(z[:, _C_CQ:_C_CQ + MLA_Q_RANK]) * gq_ref[...]
    q = jnp.dot(cq.astype(BF16), wuq_ref[...], preferred_element_type=F32)
    scale = (MLA_NOPE + MLA_ROPE) ** -0.5
    nope_w = MLA_HEADS * MLA_NOPE
    q_rope = rope(q[:, nope_w:], tmla_ref, MLA_ROPE // 2) * scale
    for hh in range(MLA_HEADS):
        qn = q[:, hh * MLA_NOPE:(hh + 1) * MLA_NOPE].astype(BF16)
        ql = jnp.dot(qn, wuk_ref[hh], preferred_element_type=F32) * scale
        qcat_ref[:, hh, :, 0:MLA_KV_RANK] = ql.reshape(bb, ts, MLA_KV_RANK).astype(BF16)
        qr = q_rope[:, hh * MLA_ROPE:(hh + 1) * MLA_ROPE]
        qcat_ref[:, hh, :, MLA_KV_RANK:MLA_QK] = qr.reshape(bb, ts, MLA_ROPE).astype(BF16)

    ckv = _rms(z[:, _C_CKV:_C_CKV + MLA_KV_RANK]) * gkv_ref[...]
    kr = rope(z[:, _C_KR:_C_KR + LANES], tkr_ref, MLA_ROPE // 2)[:, 0:MLA_ROPE]
    ckv_ref[...] = ckv.reshape(bb, ts, MLA_KV_RANK)
    krope_ref[...] = kr.reshape(bb, ts, MLA_ROPE)
    kcat_ref[:, :, 0:MLA_KV_RANK] = ckv.reshape(bb, ts, MLA_KV_RANK).astype(BF16)
    kcat_ref[:, :, MLA_KV_RANK:MLA_QK] = kr.reshape(bb, ts, MLA_ROPE).astype(BF16)

    qm = rope(z[:, _C_QM:_C_QM + MOBA_WIDTH], tmob_ref, MOBA_ROT // 2) * (MOBA_HD ** -0.5)
    km = rope(z[:, _C_KM:_C_KM + MOBA_WIDTH], tmob_ref, MOBA_ROT // 2)
    vm = z[:, _C_VM:_C_VM + MOBA_WIDTH]
    if n_mean:
        qm_ref[0] = qm.T.astype(BF16)
        vmb_ref[0] = vm.T.astype(BF16)
    else:
        qm_ref[...] = qm.reshape(bb, ts, MOBA_WIDTH).astype(BF16)
        vmb_ref[...] = vm.reshape(bb, ts, MOBA_WIDTH).astype(BF16)
    kmb_ref[...] = km.reshape(bb, ts, MOBA_WIDTH).astype(BF16)
    km_ref[...] = km.reshape(bb, ts, MOBA_WIDTH)
    vm_ref[...] = vm.reshape(bb, ts, MOBA_WIDTH)
    if n_mean:
        kmean_ref, = maybe_kmean_ref
        for i in range(n_mean):
            blk = km[i * MOBA_BLOCK:(i + 1) * MOBA_BLOCK]
            kmean_ref[0, 0, i:i + 1, :] = jnp.sum(blk, axis=0, keepdims=True) * (1.0 / MOBA_BLOCK)


def _proj(x, mod, pos, w, bb, ts, with_kmean):
    b, s, d = x.shape
    nb, nt = b // bb, s // ts
    n_mean = ts // MOBA_BLOCK if with_kmean else 0
    tmla = _rope_tables(pos, MLA_ROPE, MLA_ROPE, MLA_HEADS)
    tkr = _rope_tables(pos, MLA_ROPE, LANES, 1)
    tmob = _rope_tables(pos, MOBA_ROT, MOBA_HD, MOBA_HEADS)

    def full(a):
        return pl.BlockSpec(a.shape, lambda i, j: (0,) * a.ndim)

    def tok(wd):
        return pl.BlockSpec((bb, ts, wd), lambda i, j: (i, j, 0))

    def tab(wd):
        return pl.BlockSpec((3, ts, wd), lambda i, j: (0, j, 0))

    out_shape = [jax.ShapeDtypeStruct((b, MLA_HEADS, s, MLA_QK), BF16),
                 jax.ShapeDtypeStruct((b, s, MLA_QK), BF16),
                 jax.ShapeDtypeStruct((b, s, MLA_KV_RANK), F32),
                 jax.ShapeDtypeStruct((b, s, MLA_ROPE), F32),
                 jax.ShapeDtypeStruct((b, s, MOBA_WIDTH), BF16),
                 jax.ShapeDtypeStruct((b, s, MOBA_WIDTH), BF16),
                 jax.ShapeDtypeStruct((b, s, MOBA_WIDTH), BF16),
                 jax.ShapeDtypeStruct((b, s, MOBA_WIDTH), F32),
                 jax.ShapeDtypeStruct((b, s, MOBA_WIDTH), F32)]
    out_specs = [pl.BlockSpec((bb, MLA_HEADS, ts, MLA_QK), lambda i, j: (i, 0, j, 0)),
                 tok(MLA_QK), tok(MLA_KV_RANK), tok(MLA_ROPE),
                 tok(MOBA_WIDTH), tok(MOBA_WIDTH), tok(MOBA_WIDTH), tok(MOBA_WIDTH), tok(MOBA_WIDTH)]
    if n_mean:
        assert bb == 1
        tr_shape = jax.ShapeDtypeStruct((b, MOBA_WIDTH, s), BF16)
        tr_spec = pl.BlockSpec((1, MOBA_WIDTH, ts), lambda i, j: (i, 0, j))
        out_shape[4], out_shape[6] = tr_shape, tr_shape
        out_specs[4], out_specs[6] = tr_spec, tr_spec
        out_shape.append(jax.ShapeDtypeStruct((b, nt, n_mean, MOBA_WIDTH), F32))
        out_specs.append(pl.BlockSpec((1, 1, n_mean, MOBA_WIDTH), lambda i, j: (i, j, 0, 0)))
    args = (x, mod, w["g_mix"], w["w_in"], w["g_q"], w["w_uq"], w["g_kv"], w["w_uk_t"], tmla, tkr, tmob)
    in_specs = [tok(d), pl.BlockSpec((bb, 6, d), lambda i, j: (i, 0, 0)),
                full(w["g_mix"]), full(w["w_in"]), full(w["g_q"]), full(w["w_uq"]), full(w["g_kv"]),
                full(w["w_uk_t"]), tab(MLA_HEADS * MLA_ROPE), tab(LANES), tab(MOBA_WIDTH)]
    return pl.pallas_call(
        functools.partial(_proj_kernel, n_mean),
        grid=(nb, nt), in_specs=in_specs, out_specs=out_specs, out_shape=out_shape,
        compiler_params=_cparams(("arbitrary", "arbitrary")),
        name="proj",
    )(*args)


def _rope_tables(pos, rot, unit, rep):
    half = rot // 2
    inv = ROPE_THETA ** (-jnp.arange(half, dtype=F32) / half)
    ang = pos[:, None] * inv[None, :]
    cos, sin = jnp.cos(ang), jnp.sin(ang)
    n = pos.shape[0]
    one, zero, zh = jnp.ones((n, unit - rot), F32), jnp.zeros((n, unit - rot), F32), jnp.zeros((n, half), F32)
    c = jnp.concatenate([cos, cos, one], axis=1)
    s1 = jnp.concatenate([-sin, zh, zero], axis=1)
    s2 = jnp.concatenate([zh, sin, zero], axis=1)
    return jnp.stack([jnp.tile(t, (1, rep)) for t in (c, s1, s2)])


def _mla_kernel(tq, tk, q_ref, k_ref, wuv_ref, o_ref):
    i = pl.program_id(1)
    rows = MLA_HEADS * tq
    q = q_ref[0].reshape(rows, MLA_QK)
    q0 = i * tq

    def step(j, carry, masked):
        m, l, acc = carry
        kb = k_ref[0, pl.ds(pl.multiple_of(j * tk, tk), tk), :]
        s = lax.dot_general(q, kb, (((1,), (1,)), ((), ())), preferred_element_type=F32)
        if masked:
            qpos = q0 + lax.broadcasted_iota(I32, (MLA_HEADS, tq, tk), 1).reshape(rows, tk)
            kpos = j * tk + lax.broadcasted_iota(I32, (rows, tk), 1)
            s = jnp.where(kpos <= qpos, s, NEG)
        m_new = jnp.maximum(m, jnp.max(s, axis=-1, keepdims=True))
        alpha = jnp.exp(m - m_new)
        p = jnp.exp(s - m_new)
        l = alpha * l + jnp.sum(p, axis=-1, keepdims=True)
        acc = alpha * acc + jnp.dot(p.astype(BF16), kb[:, 0:MLA_KV_RANK], preferred_element_type=F32)
        return m_new, l, acc

    init = (jnp.full((rows, 1), NEG, F32), jnp.zeros((rows, 1), F32), jnp.zeros((rows, MLA_KV_RANK), F32))
    nfull = q0 // tk
    carry = lax.fori_loop(0, nfull, lambda j, c: step(j, c, False), init)
    m, l, acc = step(nfull, carry, True)
    o_lat = (acc / l).astype(BF16)
    out = jnp.zeros((tq, MLA_WIDTH), F32)
    for hh in range(MLA_HEADS):
        out = out + jnp.dot(o_lat[hh * tq:(hh + 1) * tq], wuv_ref[hh], preferred_element_type=F32)
    o_ref[0] = out


def _mla_prompt(qcat, kcat, wuv_pad, tq, tk):
    b, _, s, _ = qcat.shape
    return pl.pallas_call(
        functools.partial(_mla_kernel, tq, tk),
        grid=(b, s // tq),
        in_specs=[pl.BlockSpec((1, MLA_HEADS, tq, MLA_QK), lambda i, j: (i, 0, j, 0)),
                  pl.BlockSpec((1, s, MLA_QK), lambda i, j: (i, 0, 0)),
                  pl.BlockSpec(wuv_pad.shape, lambda i, j: (0, 0, 0))],
        out_specs=pl.BlockSpec((1, tq, MLA_WIDTH), lambda i, j: (i, j, 0)),
        out_shape=jax.ShapeDtypeStruct((b, s, MLA_WIDTH), F32),
        compiler_params=_cparams(("arbitrary", "arbitrary")),
        name="mla",
    )(qcat, kcat, wuv_pad)


def _moba_kernel(nb, qt_ref, k_ref, kown_ref, vt_ref, vtown_ref, kmean_ref, o_ref, sel_ref, m_ref, l_ref, acc_ref):
    i = pl.program_id(1)
    tq = MOBA_BLOCK
    row = lax.broadcasted_iota(I32, (nb, tq), 0)

    for hh in range(MOBA_HEADS):
        lo = hh * MOBA_HD
        kmean = kmean_ref[0, :, lo:lo + MOBA_HD].astype(BF16)
        gs = jnp.dot(kmean, qt_ref[0, lo:lo + MOBA_HD, :], preferred_element_type=F32)
        gs = jnp.where(row < i, gs, NEG)
        cnt = jnp.zeros((nb, tq), F32)
        for jj in range(nb):
            cj = gs[jj:jj + 1, :]
            cnt = cnt + jnp.where(cj > gs, 1.0, 0.0) + jnp.where((cj == gs) & (jj < row), 1.0, 0.0)
        sel_ref[hh] = jnp.where((cnt < MOBA_TOPK) & (row < i), 1.0, 0.0)
        m_ref[hh] = jnp.full((1, tq), NEG, F32)
        l_ref[hh] = jnp.zeros((1, tq), F32)
        acc_ref[hh] = jnp.zeros((MOBA_HD, tq), F32)

    def attend(hh, kb, vtb, keep):
        lo = hh * MOBA_HD
        s = jnp.dot(kb, qt_ref[0, lo:lo + MOBA_HD, :], preferred_element_type=F32)
        s = jnp.where(keep, s, NEG)
        m_old = m_ref[hh]
        m_new = jnp.maximum(m_old, jnp.max(s, axis=0, keepdims=True))
        alpha = jnp.exp(m_old - m_new)
        p = jnp.where(keep, jnp.exp(s - m_new), 0.0)
        l_ref[hh] = alpha * l_ref[hh] + jnp.sum(p, axis=0, keepdims=True)
        acc_ref[hh] = alpha * acc_ref[hh] + jnp.dot(vtb, p.astype(BF16), preferred_element_type=F32)
        m_ref[hh] = m_new

    for j in range(nb - 1):
        @pl.when(j < i)
        def _(j=j):
            for hh in range(MOBA_HEADS):
                lo = hh * MOBA_HD
                attend(hh, k_ref[0, j * tq:(j + 1) * tq, lo:lo + MOBA_HD],
                       vt_ref[0, lo:lo + MOBA_HD, j * tq:(j + 1) * tq], sel_ref[hh][j:j + 1, :] > 0.5)

    causal = lax.broadcasted_iota(I32, (tq, tq), 0) <= lax.broadcasted_iota(I32, (tq, tq), 1)
    for hh in range(MOBA_HEADS):
        lo = hh * MOBA_HD
        attend(hh, kown_ref[0, :, lo:lo + MOBA_HD], vtown_ref[0, lo:lo + MOBA_HD, :], causal)
    ot = jnp.concatenate([acc_ref[hh] / l_ref[hh] for hh in range(MOBA_HEADS)], axis=0)
    o_ref[0] = ot.T


def _moba_prompt(qmt, kmb, vmt, kmean):
    b, _, s = qmt.shape
    nb = s // MOBA_BLOCK
    return pl.pallas_call(
        functools.partial(_moba_kernel, nb),
        grid=(b, nb),
        in_specs=[pl.BlockSpec((1, MOBA_WIDTH, MOBA_BLOCK), lambda i, j: (i, 0, j)),
                  pl.BlockSpec((1, s, MOBA_WIDTH), lambda i, j: (i, 0, 0)),
                  pl.BlockSpec((1, MOBA_BLOCK, MOBA_WIDTH), lambda i, j: (i, j, 0)),
                  pl.BlockSpec((1, MOBA_WIDTH, s), lambda i, j: (i, 0, 0)),
                  pl.BlockSpec((1, MOBA_WIDTH, MOBA_BLOCK), lambda i, j: (i, 0, j)),
                  pl.BlockSpec((1, nb, MOBA_WIDTH), lambda i, j: (i, 0, 0))],
        out_specs=pl.BlockSpec((1, MOBA_BLOCK, MOBA_WIDTH), lambda i, j: (i, j, 0)),
        out_shape=jax.ShapeDtypeStruct((b, s, MOBA_WIDTH), F32),
        scratch_shapes=[pltpu.VMEM((MOBA_HEADS, nb, MOBA_BLOCK), F32),
                        pltpu.VMEM((MOBA_HEADS, 1, MOBA_BLOCK), F32),
                        pltpu.VMEM((MOBA_HEADS, 1, MOBA_BLOCK), F32),
                        pltpu.VMEM((MOBA_HEADS, MOBA_HD, MOBA_BLOCK), F32)],
        compiler_params=_cparams(("arbitrary", "arbitrary")),
        name="moba",
    )(qmt, kmb, kmb, vmt, vmt, kmean)


def _mla_dec_kernel(pps, t_new, pt_ref, q_ref, knew_ref, wuv_ref, *rest):
    ckv_refs, kr_refs = rest[:pps], rest[pps:2 * pps]
    o_ref, m_ref, l_ref, acc_ref = rest[2 * pps:]
    c = pl.program_id(1)
    rows = MLA_HEADS * t_new
    q = q_ref[0].reshape(rows, MLA_QK)
    ql, qr = q[:, 0:MLA_KV_RANK], q[:, MLA_KV_RANK:MLA_QK]

    @pl.when(c == 0)
    def _():
        m_ref[...] = jnp.full(m_ref.shape, NEG, F32)
        l_ref[...] = jnp.zeros(l_ref.shape, F32)
        acc_ref[...] = jnp.zeros(acc_ref.shape, F32)

    nt = (((1,), (1,)), ((), ()))
    kc = jnp.concatenate([r[0].astype(BF16) for r in ckv_refs], axis=0)
    krt = jnp.concatenate([r[0].astype(BF16) for r in kr_refs], axis=1)
    s = (lax.dot_general(ql, kc, nt, preferred_element_type=F32)
         + jnp.dot(qr, krt, preferred_element_type=F32))
    m_old = m_ref[...]
    m_new = jnp.maximum(m_old, jnp.max(s, axis=-1, keepdims=True))
    alpha = jnp.exp(m_old - m_new)
    p = jnp.exp(s - m_new)
    l_ref[...] = alpha * l_ref[...] + jnp.sum(p, axis=-1, keepdims=True)
    acc_ref[...] = alpha * acc_ref[...] + jnp.dot(p.astype(BF16), kc, preferred_element_type=F32)
    m_ref[...] = m_new

    @pl.when(c == pl.num_programs(1) - 1)
    def _():
        kn = knew_ref[0]
        sn = lax.dot_general(q, kn, nt, preferred_element_type=F32)
        tq = lax.broadcasted_iota(I32, (MLA_HEADS, t_new, t_new), 1).reshape(rows, t_new)
        tk = lax.broadcasted_iota(I32, (rows, t_new), 1)
        sn = jnp.where(tk <= tq, sn, NEG)
        m1 = m_ref[...]
        m2 = jnp.maximum(m1, jnp.max(sn, axis=-1, keepdims=True))
        a2 = jnp.exp(m1 - m2)
        pn = jnp.exp(sn - m2)
        l2 = a2 * l_ref[...] + jnp.sum(pn, axis=-1, keepdims=True)
        acc2 = a2 * acc_ref[...] + jnp.dot(pn.astype(BF16), kn[:, 0:MLA_KV_RANK], preferred_element_type=F32)
        o_lat = (acc2 / l2).astype(BF16)
        out = jnp.zeros((t_new, MLA_WIDTH), F32)
        for hh in range(MLA_HEADS):
            out = out + jnp.dot(o_lat[hh * t_new:(hh + 1) * t_new], wuv_ref[hh], preferred_element_type=F32)
        o_ref[0] = out


def _mla_sample(qcat, kcat_new, wuv_pad, cache_ckv, cache_kr, page_table, pps):
    b, _, t_new, _ = qcat.shape
    n_pages = page_table.shape[1]
    rows = MLA_HEADS * t_new

    def page_spec(k, shape):
        return pl.BlockSpec((1,) + shape, lambda i, c, pt: (pt[i, c * pps + k], 0, 0))

    in_specs = ([pl.BlockSpec((1, MLA_HEADS, t_new, MLA_QK), lambda i, c, pt: (i, 0, 0, 0)),
                 pl.BlockSpec((1, t_new, MLA_QK), lambda i, c, pt: (i, 0, 0)),
                 pl.BlockSpec(wuv_pad.shape, lambda i, c, pt: (0, 0, 0))]
                + [page_spec(k, (PAGE_SIZE, MLA_KV_RANK)) for k in range(pps)]
                + [page_spec(k, (MLA_ROPE, PAGE_SIZE)) for k in range(pps)])
    grid_spec = pltpu.PrefetchScalarGridSpec(
        num_scalar_prefetch=1, grid=(b, n_pages // pps), in_specs=in_specs,
        out_specs=pl.BlockSpec((1, t_new, MLA_WIDTH), lambda i, c, pt: (i, 0, 0)),
        scratch_shapes=[pltpu.VMEM((rows, 1), F32), pltpu.VMEM((rows, 1), F32),
                        pltpu.VMEM((rows, MLA_KV_RANK), F32)])
    return pl.pallas_call(
        functools.partial(_mla_dec_kernel, pps, t_new),
        grid_spec=grid_spec,
        out_shape=jax.ShapeDtypeStruct((b, t_new, MLA_WIDTH), F32),
        compiler_params=_cparams(("arbitrary", "arbitrary")),
        name="mla_dec",
    )(page_table, qcat, kcat_new, wuv_pad, *([cache_ckv] * pps), *([cache_kr] * pps))


def _moba_dec_kernel(pps, t_new, n_pages, pt_ref, q_ref, knew_ref, vnew_ref, *rest):
    k_refs, v_refs = rest[:pps], rest[pps:2 * pps]
    o_ref, qbd_ref, s_ref, p_ref, gs_ref, l_ref, own_ref, acc_ref = rest[2 * pps:]
    phase, c = pl.program_id(1), pl.program_id(2)
    nc = n_pages // pps
    rows = MOBA_HEADS * t_new
    n_blk = n_pages * PAGE_SIZE // MOBA_BLOCK
    bps = pps * PAGE_SIZE // MOBA_BLOCK
    nt = (((1,), (1,)), ((), ()))
    bcol = lax.broadcasted_iota(I32, (rows, n_blk), 1)

    @pl.when((phase == 0) & (c == 0))
    def _():
        qt = jnp.concatenate([q_ref[0].astype(F32)] * MOBA_HEADS, axis=0)
        rh = lax.broadcasted_iota(I32, (MOBA_HEADS, t_new, MOBA_WIDTH), 0).reshape(rows, MOBA_WIDTH)
        ch = lax.broadcasted_iota(I32, (rows, MOBA_HEADS, MOBA_HD), 1).reshape(rows, MOBA_WIDTH)
        qbd_ref[...] = jnp.where(rh == ch, qt, 0.0).astype(BF16)
        gs_ref[...] = jnp.zeros(gs_ref.shape, F32)

    @pl.when(phase == 0)
    def _():
        kt = jnp.concatenate([r[0].astype(BF16) for r in k_refs], axis=1)
        s = jnp.dot(qbd_ref[...], kt, preferred_element_type=F32)
        s_ref[c] = s
        gs = gs_ref[...]
        for kb in range(bps):
            gsum = jnp.sum(s[:, kb * MOBA_BLOCK:(kb + 1) * MOBA_BLOCK], axis=-1, keepdims=True)
            gs = jnp.where(bcol == c * bps + kb, gsum, gs)
        gs_ref[...] = gs

    @pl.when((phase == 0) & (c == nc - 1))
    def _():
        qbd = qbd_ref[...]
        gs = gs_ref[...]
        sel = jnp.zeros((rows, n_blk), F32)
        for _ in range(min(MOBA_TOPK, n_blk)):
            mx = jnp.max(gs, axis=-1, keepdims=True)
            first = jnp.min(jnp.where(gs == mx, bcol, n_blk), axis=-1, keepdims=True)
            hit = bcol == first
            sel = jnp.where(hit, 1.0, sel)
            gs = jnp.where(hit, -jnp.inf, gs)
        kn = knew_ref[0].astype(BF16)
        so = lax.dot_general(qbd, kn, nt, preferred_element_type=F32)
        tq = lax.broadcasted_iota(I32, (MOBA_HEADS, t_new, t_new), 1).reshape(rows, t_new)
        tk = lax.broadcasted_iota(I32, (rows, t_new), 1)
        so = jnp.where(tk <= tq, so, NEG)

        def block_scores(blk):
            lo = (blk % bps) * MOBA_BLOCK
            return s_ref[blk // bps, :, lo:lo + MOBA_BLOCK], sel[:, blk:blk + 1] > 0.5

        half = MOBA_BLOCK // 2
        m_el = jnp.full((rows, half), NEG, F32)
        for blk in range(n_blk):
            sb, keep = block_scores(blk)
            m_el = jnp.maximum(m_el, jnp.where(keep, jnp.maximum(sb[:, :half], sb[:, half:]), NEG))
        m = jnp.maximum(jnp.max(m_el, axis=-1, keepdims=True), jnp.max(so, axis=-1, keepdims=True))
        l_el = jnp.zeros((rows, half), F32)
        for blk in range(n_blk):
            sb, keep = block_scores(blk)
            p = jnp.where(keep, jnp.exp(jnp.where(keep, sb, NEG) - m), 0.0)
            l_el = l_el + p[:, :half] + p[:, half:]
            lo = (blk % bps) * MOBA_BLOCK
            p_ref[blk // bps, :, lo:lo + MOBA_BLOCK] = p.astype(BF16)
        po = jnp.exp(so - m)
        l_ref[...] = jnp.sum(l_el, axis=-1, keepdims=True) + jnp.sum(po, axis=-1, keepdims=True)
        own_ref[...] = jnp.dot(po.astype(BF16), vnew_ref[0].astype(BF16), preferred_element_type=F32)
        acc_ref[...] = jnp.zeros(acc_ref.shape, F32)

    @pl.when(phase == 1)
    def _():
        vt = jnp.concatenate([r[0].astype(BF16) for r in v_refs], axis=1)
        acc_ref[...] += lax.dot_general(p_ref[c], vt, nt, preferred_element_type=F32)

    @pl.when((phase == 1) & (c == nc - 1))
    def _():
        full = (acc_ref[...] + own_ref[...]) / l_ref[...]
        for hh in range(MOBA_HEADS):
            o_ref[0, :, hh * MOBA_HD:(hh + 1) * MOBA_HD] = full[hh * t_new:(hh + 1) * t_new,
                                                                 hh * MOBA_HD:(hh + 1) * MOBA_HD]


def _moba_sample(qm, km_new, vm_new, cache_k, cache_v, page_table, pps):
    b, t_new, _ = qm.shape
    n_pages = page_table.shape[1]
    nc = n_pages // pps
    rows = MOBA_HEADS * t_new
    n_blk = n_pages * PAGE_SIZE // MOBA_BLOCK

    def k_spec(k):
        return pl.BlockSpec((1, MOBA_WIDTH, PAGE_SIZE),
                            lambda i, ph, c, pt: (pt[i, jnp.where(ph == 0, c, nc - 1) * pps + k], 0, 0))

    def v_spec(k):
        return pl.BlockSpec((1, MOBA_WIDTH, PAGE_SIZE),
                            lambda i, ph, c, pt: (pt[i, jnp.where(ph == 0, 0, c) * pps + k], 0, 0))

    tok = pl.BlockSpec((1, t_new, MOBA_WIDTH), lambda i, ph, c, pt: (i, 0, 0))
    grid_spec = pltpu.PrefetchScalarGridSpec(
        num_scalar_prefetch=1, grid=(b, 2, nc),
        in_specs=[tok, tok, tok] + [k_spec(k) for k in range(pps)] + [v_spec(k) for k in range(pps)],
        out_specs=tok,
        scratch_shapes=[pltpu.VMEM((rows, MOBA_WIDTH), BF16),
                        pltpu.VMEM((nc, rows, pps * PAGE_SIZE), F32),
                        pltpu.VMEM((nc, rows, pps * PAGE_SIZE), BF16),
                        pltpu.VMEM((rows, n_blk), F32),
                        pltpu.VMEM((rows, 1), F32),
                        pltpu.VMEM((rows, MOBA_WIDTH), F32),
                        pltpu.VMEM((rows, MOBA_WIDTH), F32)])
    return pl.pallas_call(
        functools.partial(_moba_dec_kernel, pps, t_new, n_pages),
        grid_spec=grid_spec,
        out_shape=jax.ShapeDtypeStruct((b, t_new, MOBA_WIDTH), F32),
        compiler_params=_cparams(("arbitrary", "arbitrary", "arbitrary")),
        name="moba_dec",
    )(page_table, qm, km_new, vm_new, *([cache_k] * pps), *([cache_v] * pps))


def _oproj_kernel(x_ref, mod_ref, omla_ref, omoba_ref, bmla_ref, bmoba_ref, wo_ref, gffn_ref, wpq_ref,
                  x1_ref, h_ref, qp_ref):
    bb, ts, d = x_ref.shape
    m = bb * ts
    mod = mod_ref[...]
    a = (_rms(omla_ref[...]) * bmla_ref[...]).reshape(m, MLA_WIDTH).astype(BF16)
    b = (_rms(omoba_ref[...]) * bmoba_ref[...]).reshape(m, MOBA_WIDTH).astype(BF16)
    o = jnp.concatenate([a, b], axis=-1)
    y = jnp.dot(o, wo_ref[...], preferred_element_type=F32).reshape(bb, ts, d)
    x1 = x_ref[...] + mod[:, 2:3, :] * y
    x1_ref[...] = x1
    h = (_rms(x1) * gffn_ref[...] * (1.0 + mod[:, 4:5, :]) + mod[:, 3:4, :]).astype(BF16)
    h_ref[...] = h
    qp = jnp.dot(h.reshape(m, d), wpq_ref[...], preferred_element_type=F32)
    qp_ref[...] = qp.reshape(bb, ts, qp.shape[-1]).astype(BF16)


def _oproj(x, mod, o_mla, o_moba, w, bb, ts):
    b, s, d = x.shape
    pq = w["w_pq"].shape[1]

    def full(a):
        return pl.BlockSpec(a.shape, lambda i, j: (0,) * a.ndim)

    def tok(wd):
        return pl.BlockSpec((bb, ts, wd), lambda i, j: (i, j, 0))

    return pl.pallas_call(
        _oproj_kernel,
        grid=(b // bb, s // ts),
        in_specs=[tok(d), pl.BlockSpec((bb, 6, d), lambda i, j: (i, 0, 0)), tok(MLA_WIDTH), tok(MOBA_WIDTH),
                  full(w["beta_mla"]), full(w["beta_moba"]), full(w["w_o"]), full(w["g_ffn"]), full(w["w_pq"])],
        out_specs=[tok(d), tok(d), tok(pq)],
        out_shape=[jax.ShapeDtypeStruct((b, s, d), F32), jax.ShapeDtypeStruct((b, s, d), BF16),
                   jax.ShapeDtypeStruct((b, s, pq), BF16)],
        compiler_params=_cparams(("arbitrary", "arbitrary")),
        name="oproj",
    )(x, mod, o_mla, o_moba, w["beta_mla"], w["beta_moba"], w["w_o"], w["g_ffn"], w["w_pq"])


def _top_rows(s, k, payload=None):
    n = s.shape[0]
    row = lax.broadcasted_iota(I32, s.shape, 0)
    vals, idxs, picks = [], [], []
    for _ in range(k):
        mx = jnp.max(s, axis=0, keepdims=True)
        first = jnp.min(jnp.where(s == mx, row, n), axis=0, keepdims=True)
        hit = row == first
        vals.append(mx)
        idxs.append(first)
        if payload is not None:
            picks.append(jnp.max(jnp.where(hit, payload, -1), axis=0, keepdims=True))
        s = jnp.where(hit, -jnp.inf, s)
    cat = lambda xs: jnp.concatenate(xs, axis=0)
    return cat(vals), cat(idxs), (cat(picks) if payload is not None else None)


def _route_kernel(qp_ref, keys_ref, idx_ref, g_ref):
    nt = (((1,), (1,)), ((), ()))
    half = PEER_DKEY // 2
    for hh in range(PEER_HEADS):
        tops = []
        for p in range(2):
            lo = (hh * 2 + p) * half
            s = lax.dot_general(keys_ref[hh * 2 + p], qp_ref[:, lo:lo + half], nt,
                                preferred_element_type=F32)
            v, ix, _ = _top_rows(s, PEER_TOPK)
            tops.append((v, ix))
        (s1, i1), (s2, i2) = tops
        slabs_s, slabs_i = [], []
        for a in range(PEER_TOPK):
            nb = PEER_TOPK // (a + 1)
            keep = PEER_TOPK if nb > 8 else 8
            ss = s1[a:a + 1] + s2[0:keep]
            if nb < keep:
                ss = jnp.where(lax.broadcasted_iota(I32, ss.shape, 0) < nb, ss, -jnp.inf)
            slabs_s.append(ss)
            slabs_i.append(i1[a:a + 1] * PEER_NKEYS + i2[0:keep])
        cand_s = jnp.concatenate(slabs_s, axis=0)
        cand_i = jnp.concatenate(slabs_i, axis=0)
        top_s, _, top_i = _top_rows(cand_s, PEER_TOPK, cand_i)
        e = jnp.exp(top_s - top_s[0:1])
        g_ref[hh * PEER_TOPK:(hh + 1) * PEER_TOPK, :] = e / jnp.sum(e, axis=0, keepdims=True)
        idx_ref[hh * PEER_TOPK:(hh + 1) * PEER_TOPK, :] = top_i


def _route(qp, keys, tm):
    n, pq = qp.shape
    return pl.pallas_call(
        _route_kernel,
        grid=(n // tm,),
        in_specs=[pl.BlockSpec((tm, pq), lambda i: (i, 0)),
                  pl.BlockSpec(keys.shape, lambda i: (0, 0, 0))],
        out_specs=[pl.BlockSpec((PEER_SEL, tm), lambda i: (0, i)), pl.BlockSpec((PEER_SEL, tm), lambda i: (0, i))],
        out_shape=[jax.ShapeDtypeStruct((PEER_SEL, n), I32), jax.ShapeDtypeStruct((PEER_SEL, n), F32)],
        compiler_params=_cparams(("arbitrary",)),
        name="route",
    )(qp, keys)


def _gelu_tanh(x):
    return 0.5 * x * (1.0 + jnp.tanh(0.7978845608028654 * (x + 0.044715 * (x * x * x))))


def _expert_kernel(tm, ec, h_ref, idx_ref, g_ref, u_ref, v_ref, o_ref, gate_ref):
    j = pl.program_id(1)
    nt = (((1,), (1,)), ((), ()))
    sub = ec // PEER_NKEYS

    @pl.when(j == 0)
    def _():
        rows = lax.broadcasted_iota(I32, (PEER_NKEYS, PEER_SEL), 0)
        grp = next(g for g in (32, 16, 8) if tm % g == 0)

        def tok8(t, carry):
            base = pl.multiple_of(t * grp, grp)
            ix8 = idx_ref[pl.ds(base, grp), :]
            g8 = g_ref[pl.ds(base, grp), :]
            for r in range(grp):
                ix, gg = ix8[r:r + 1, :], g8[r:r + 1, :]
                w1 = jnp.where(rows == (ix >> 7), gg, 0.0).astype(BF16)
                w2 = jnp.where(rows == (ix & (PEER_NKEYS - 1)), 1.0, 0.0).astype(BF16)
                gate_ref[pl.ds(pl.multiple_of((base + r) * GATE_PITCH, 8), PEER_NKEYS), :] = lax.dot_general(
                    w1, w2, nt, preferred_element_type=F32)
            return carry

        lax.fori_loop(0, tm // grp, tok8, 0)
        o_ref[...] = jnp.zeros(o_ref.shape, F32)

    a = jnp.dot(h_ref[...], u_ref[...], preferred_element_type=F32)
    ws = []
    for e in range(sub):
        gate = gate_ref[pl.ds(j * sub + e, tm, stride=GATE_PITCH), :]
        ws.append((gate * _gelu_tanh(a[:, e * PEER_NKEYS:(e + 1) * PEER_NKEYS])).astype(BF16))
    w = jnp.concatenate(ws, axis=-1)
    o_ref[...] += jnp.dot(w, v_ref[...], preferred_element_type=F32)


def _expert(h, idx, g, u_tab_t, v_tab, tm, ec):
    n, d = h.shape
    ne = v_tab.shape[0]
    return pl.pallas_call(
        functools.partial(_expert_kernel, tm, ec),
        grid=(n // tm, ne // ec),
        in_specs=[pl.BlockSpec((tm, d), lambda i, j: (i, 0)),
                  pl.BlockSpec((tm, PEER_SEL), lambda i, j: (i, 0)),
                  pl.BlockSpec((tm, PEER_SEL), lambda i, j: (i, 0)),
                  pl.BlockSpec((d, ec), lambda i, j: (0, j)),
                  pl.BlockSpec((ec, d), lambda i, j: (j, 0))],
        out_specs=pl.BlockSpec((tm, d), lambda i, j: (i, 0)),
        out_shape=jax.ShapeDtypeStruct((n, d), F32),
        scratch_shapes=[pltpu.VMEM((tm * GATE_PITCH, PEER_NKEYS), F32)],
        compiler_params=_cparams(("arbitrary", "arbitrary")),
        name="expert",
    )(h, idx, g, u_tab_t, v_tab)


def _final_kernel(last, x_ref, mod_ref, f_ref, gfin_ref, o_ref):
    x2 = x_ref[...] + mod_ref[...][:, 5:6, :] * f_ref[...]
    o_ref[...] = _rms(x2) * gfin_ref[...] if last else x2


def _final(x1, mod, f, g_final, bb, ts, last):
    b, s, d = x1.shape
    tok = pl.BlockSpec((bb, ts, d), lambda i, j: (i, j, 0))
    return pl.pallas_call(
        functools.partial(_final_kernel, last),
        grid=(b // bb, s // ts),
        in_specs=[tok, pl.BlockSpec((bb, 6, d), lambda i, j: (i, 0, 0)), tok,
                  pl.BlockSpec(g_final.shape, lambda i, j: (0, 0))],
        out_specs=tok,
        out_shape=jax.ShapeDtypeStruct((b, s, d), F32),
        compiler_params=_cparams(("arbitrary", "arbitrary")),
        name="final",
    )(x1, mod, f, g_final)


def _prep_weights(l, g_mix, w_in, g_q, w_uq, g_kv, w_uk, w_uv, beta_mla, beta_moba, w_o, g_ffn, w_pq,
                  sub_keys, u_tab, v_tab):
    d = w_in.shape[1]
    o_cq, o_ckv, o_kr = 0, MLA_Q_RANK, MLA_Q_RANK + MLA_KV_RANK
    o_qm = o_kr + MLA_ROPE
    o_km, o_vm = o_qm + MOBA_WIDTH, o_qm + 2 * MOBA_WIDTH
    wi = w_in[l]
    w_in_r = jnp.concatenate(
        [wi[:, o_cq:o_ckv], wi[:, o_ckv:o_kr], wi[:, o_qm:o_km], wi[:, o_km:o_vm], wi[:, o_vm:o_vm + MOBA_WIDTH],
         wi[:, o_kr:o_qm], jnp.zeros((d, LANES - MLA_ROPE), wi.dtype)], axis=1).astype(BF16)
    wq = w_uq[l].reshape(MLA_Q_RANK, MLA_HEADS, MLA_NOPE + MLA_ROPE)
    w_uq_r = jnp.concatenate([wq[:, :, :MLA_NOPE].reshape(MLA_Q_RANK, -1),
                              wq[:, :, MLA_NOPE:].reshape(MLA_Q_RANK, -1)], axis=1).astype(BF16)
    w_uk_t = jnp.transpose(w_uk[l], (1, 2, 0)).astype(BF16)
    wv = jnp.transpose(w_uv[l], (1, 0, 2))
    eye = jnp.eye(MLA_HEADS, dtype=wv.dtype)
    wuv_pad = (wv[:, :, None, :] * eye[:, None, :, None]).reshape(MLA_HEADS, MLA_KV_RANK, MLA_WIDTH).astype(BF16)
    row = lambda a: a[l].reshape(1, -1)
    return dict(
        g_mix=row(g_mix), w_in=w_in_r, g_q=row(g_q), w_uq=w_uq_r, g_kv=row(g_kv), w_uk_t=w_uk_t,
        wuv_pad=wuv_pad, beta_mla=row(beta_mla), beta_moba=row(beta_moba), w_o=w_o[l].astype(BF16),
        g_ffn=row(g_ffn), w_pq=w_pq[l].astype(BF16),
        keys=sub_keys[l].reshape(PEER_HEADS * 2, PEER_NKEYS, PEER_DKEY // 2).astype(BF16),
        u_tab_t=u_tab[l].T.astype(BF16), v_tab=v_tab[l].astype(BF16))


def _pick(n, pref):
    t = min(n, pref)
    while n % t:
        t -= 8
    return t


def kernel(x_prompt, x_sample, cache_mla_ckv, cache_mla_krope, cache_moba_k, cache_moba_v, page_table, c_prompt, c_sample, w_ada, b_ada, g_mix, w_in, g_q, w_uq, g_kv, w_uk, w_uv, beta_mla, beta_moba, w_o, g_ffn, w_pq, sub_keys, u_tab, v_tab, g_final):
    bp, sp, d = x_prompt.shape
    bs, ts, _ = x_sample.shape
    depth = w_ada.shape[0]
    n_pages = page_table.shape[1]
    past = n_pages * PAGE_SIZE
    assert sp % MOBA_BLOCK == 0 and past % MOBA_BLOCK == 0 and ts % 8 == 0
    pos_p = jnp.arange(sp, dtype=F32)
    pos_s = past + jnp.arange(ts, dtype=F32)
    n_pool = cache_moba_k.shape[1]
    gfin = g_final.reshape(1, d)

    xp, xs = x_prompt, x_sample
    new_p, new_s = [], []
    for l in range(depth):
        w = _prep_weights(l, g_mix, w_in, g_q, w_uq, g_kv, w_uk, w_uv, beta_mla, beta_moba, w_o, g_ffn, w_pq,
                          sub_keys, u_tab, v_tab)
        mod = _ada(jnp.concatenate([c_prompt, c_sample], axis=0), w_ada[l], b_ada[l])
        mod_p = mod[:bp].reshape(bp, 6, d)
        mod_s = mod[bp:].reshape(bs, 6, d)

        tp = _pick(sp, 512)
        qcat, kcat, ckv, krope, qm, kmb, vmb, km, vm, kmean = _proj(xp, mod_p, pos_p, w, 1, tp, True)
        o_mla = _mla_prompt(qcat, kcat, w["wuv_pad"], _pick(sp, 128), _pick(sp, 256))
        o_moba = _moba_prompt(qm, kmb, vmb, kmean.reshape(bp, sp // MOBA_BLOCK, MOBA_WIDTH))
        x1_p, h_p, qp_p = _oproj(xp, mod_p, o_mla, o_moba, w, 1, tp)
        new_p.append((ckv, krope, km.reshape(bp, sp, MOBA_HEADS, MOBA_HD), vm.reshape(bp, sp, MOBA_HEADS, MOBA_HD)))

        bbs = _pick(bs, 16)
        qcat_s, kcat_s, ckv_s, krope_s, qm_s, _, _, km_s, vm_s = _proj(xs, mod_s, pos_s, w, bbs, ts, False)
        kr_t = jnp.transpose(cache_mla_krope[l], (0, 2, 1))
        k_t = jnp.transpose(cache_moba_k[l], (0, 2, 3, 1)).reshape(n_pool, MOBA_WIDTH, PAGE_SIZE)
        v_t = jnp.transpose(cache_moba_v[l], (0, 2, 3, 1)).reshape(n_pool, MOBA_WIDTH, PAGE_SIZE)
        o_mla_s = _mla_sample(qcat_s, kcat_s, w["wuv_pad"], cache_mla_ckv[l], kr_t, page_table,
                              _pick(n_pages, 32))
        o_moba_s = _moba_sample(qm_s, km_s, vm_s, k_t, v_t, page_table, _pick(n_pages, 32))
        x1_s, h_s, qp_s = _oproj(xs, mod_s, o_mla_s, o_moba_s, w, bbs, ts)
        new_s.append((ckv_s, krope_s, km_s.reshape(bs, ts, MOBA_HEADS, MOBA_HD),
                      vm_s.reshape(bs, ts, MOBA_HEADS, MOBA_HD)))

        def peer(h, qp):
            n = h.shape[0] * h.shape[1]
            tr = _pick(n, 256)
            idx_t, g_t = _route(qp.reshape(n, -1), w["keys"], tr)
            f = _expert(h.reshape(n, d), idx_t.T, g_t.T, w["u_tab_t"], w["v_tab"], tr,
                        _pick(w["v_tab"].shape[0], 2048))
            return f.reshape(h.shape)

        last = l == depth - 1
        xp = _final(x1_p, mod_p, peer(h_p, qp_p), gfin, 1, tp, last)
        xs = _final(x1_s, mod_s, peer(h_s, qp_s), gfin, bbs, ts, last)

    def stack(sts, i):
        return jnp.stack([st[i] for st in sts])

    return (xp, xs, stack(new_p, 0), stack(new_p, 1), stack(new_p, 2), stack(new_p, 3),
            stack(new_s, 0), stack(new_s, 1), stack(new_s, 2), stack(new_s, 3))
```
